```python
import jax, jax.numpy as jnp
from jax import lax
import numpy as np

D_MODEL = 1024
BATCH = 8
SEQ = 4096
DEPTH = 2

GRID_W = 64
CTX_LEN = 256
N_BRANCH = 4
BR_W = D_MODEL // 4
HEAD_DIM = 64
N_HEADS = BR_W // HEAD_DIM
CHUNK = 128
NA_ROWS = 8
NA_COLS = 16
LORA_W = 64
LORA_A = 64
RW_IN = 3 * BR_W + LORA_W + LORA_A
POOL_WINDOWS = (2, 4, 8, 16)
EPS = 1e-6
GN_EPS = 64e-5
IN_COLS = 10 * BR_W + RW_IN + N_BRANCH * D_MODEL
SEGMENTS = (("gm_u", BR_W), ("gm_v", BR_W), ("gm_g", BR_W),
            ("na_q", BR_W), ("na_k", BR_W), ("na_v", BR_W), ("na_g", BR_W),
            ("rw_in", RW_IN), ("rw_g", BR_W),
            ("pl_p", BR_W), ("pl_g", BR_W),
            ("merge", N_BRANCH * D_MODEL))

kernel_name = "hybrid_gmlp_natten_rwkv7_pool_dit"


def rmsnorm(x, g):
    xf = x.astype(jnp.float32)
    y = xf * lax.rsqrt(jnp.mean(xf * xf, axis=-1, keepdims=True) + EPS)
    return (y * g).astype(x.dtype)


def to_heads(t):
    return t.reshape(t.shape[:-1] + (N_HEADS, HEAD_DIM))


def split_cols(z):
    out = {}
    off = 0
    for name, width in SEGMENTS:
        out[name] = z[..., off:off + width]
        off += width
    return out


def gmlp_mix(u, v, ln_g, ln_b, w_s, b_s):
    B, T, _ = u.shape
    u = jax.nn.gelu(u)
    vh = to_heads(jax.nn.gelu(v)).astype(jnp.float32)
    mu = jnp.mean(vh, axis=-1, keepdims=True)
    var = jnp.mean(jnp.square(vh - mu), axis=-1, keepdims=True)
    vh = ((vh - mu) * lax.rsqrt(var + EPS) * ln_g + ln_b).astype(u.dtype)
    vc = vh.reshape(B, T // CHUNK, CHUNK, N_HEADS, HEAD_DIM)
    mixed = jnp.einsum('gpq,bnqgc->bnpgc', w_s, vc) + b_s.T[None, None, :, :, None]
    return u * mixed.reshape(B, T, BR_W)


def natten_latent(q, k, v, k_ctx, v_ctx, rpb):
    B, T, H, Dh = q.shape
    rows = T // GRID_W
    kr = min(NA_ROWS, rows)
    r_idx = jnp.arange(rows)
    c_idx = jnp.arange(GRID_W)
    r_start = jnp.clip(r_idx - kr // 2, 0, rows - kr)
    c_start = jnp.clip(c_idx - NA_COLS // 2, 0, GRID_W - NA_COLS)
    band_rows = r_start[:, None] + jnp.arange(kr)[None, :]
    qg = q.reshape(B, rows, GRID_W, H, Dh)
    kg = k.reshape(B, rows, GRID_W, H, Dh)[:, band_rows]
    vg = v.reshape(B, rows, GRID_W, H, Dh)[:, band_rows]
    scale = Dh ** -0.5
    s_band = jnp.einsum('brqhd,brikhd->bhrqik', qg, kg).astype(jnp.float32) * scale
    col_ok = (c_idx[None, :] >= c_start[:, None]) & (c_idx[None, :] < c_start[:, None] + NA_COLS)
    dr = band_rows - r_idx[:, None]
    dc = jnp.clip(c_idx[None, :] - c_idx[:, None], -(NA_COLS - 1), NA_COLS - 1)
    bias = rpb[:, dr[:, None, :, None] + (NA_ROWS - 1), dc[None, :, None, :] + (NA_COLS - 1)]
    s_band = jnp.where(col_ok[None, None, None, :, None, :], s_band + bias[None].astype(jnp.float32), -jnp.inf)
    s_ctx = jnp.einsum('brqhd,blhd->bhrql', qg, k_ctx).astype(jnp.float32) * scale
    s_all = jnp.concatenate([s_band.reshape(B, H, rows, GRID_W, kr * GRID_W), s_ctx], axis=-1)
    p = jax.nn.softmax(s_all, axis=-1).astype(v.dtype)
    p_band = p[..., :kr * GRID_W].reshape(B, H, rows, GRID_W, kr, GRID_W)
    p_ctx = p[..., kr * GRID_W:]
    o = jnp.einsum('bhrqik,brikhd->brqhd', p_band, vg) + jnp.einsum('bhrql,blhd->brqhd', p_ctx, v_ctx)
    return o.reshape(B, T, H * Dh)


def ctx_attention(q, k, v):
    B, L, H, Dh = q.shape
    s = jnp.einsum('blhd,bmhd->bhlm', q, k).astype(jnp.float32) * (Dh ** -0.5)
    p = jax.nn.softmax(s, axis=-1).astype(v.dtype)
    return jnp.einsum('bhlm,bmhd->blhd', p, v).reshape(B, L, H * Dh)


def centred_shift_mix(z, mu):
    prev = jnp.pad(z[:, :-1], ((0, 0), (1, 0), (0, 0)))
    nxt = jnp.pad(z[:, 1:], ((0, 0), (0, 1), (0, 0)))
    return z + (0.5 * (prev + nxt) - z) * mu


def wkv7_scan(r, w, k, v, a, b, s0, reverse):
    def step(S, inp):
        r_t, w_t, k_t, v_t, a_t, b_t = inp
        sa = jnp.einsum('bhvk,bhk->bhv', S, a_t)
        S = S * w_t[:, :, None, :] + sa[..., :, None] * b_t[..., None, :] + v_t[..., :, None] * k_t[..., None, :]
        return S, jnp.einsum('bhvk,bhk->bhv', S, r_t)
    xs = tuple(jnp.moveaxis(t, 1, 0) for t in (r, w, k, v, a, b))
    s_fin, ys = lax.scan(step, s0, xs, reverse=reverse)
    return jnp.moveaxis(ys, 0, 1), s_fin


def rwkv_branch(z_in, s0_f, s0_b, lp):
    dt = z_in.dtype
    B, T, _ = z_in.shape
    zf = centred_shift_mix(z_in, lp['rw_mu']).astype(jnp.float32)
    r, k, v, w_lo, a_lo = jnp.split(zf, [BR_W, 2 * BR_W, 3 * BR_W, 3 * BR_W + LORA_W], axis=-1)
    kk = to_heads(k * lp['rw_kk'])
    kk = kk / jnp.maximum(jnp.sqrt(jnp.sum(kk * kk, axis=-1, keepdims=True)), 1e-12)
    rh, vh, kh = to_heads(r), to_heads(v), to_heads(k)
    r_k = lp['rw_rk'].astype(jnp.float32)
    k_a = to_heads(lp['rw_ka'])
    s0 = (s0_f, s0_b)
    ys, finals, bonus = [], [], []
    for d in range(2):
        w = -jax.nn.softplus(-(lp['rw_w0'][d] + jnp.tanh(w_lo) @ lp['rw_w2'][d])) - 0.5
        decay = to_heads(jnp.exp(-jnp.exp(w)))
        a = to_heads(jax.nn.sigmoid(lp['rw_a0'][d] + a_lo @ lp['rw_a2'][d]))
        kd = kh * (1.0 + (a - 1.0) * k_a)
        y, s_fin = wkv7_scan(rh, decay, kd, vh, -kk, kk * a, s0[d], reverse=(d == 1))
        ys.append(y)
        finals.append(s_fin)
        bonus.append(jnp.sum(rh * kd * r_k, axis=-1, keepdims=True) * vh)
    o = ys[0] + ys[1]
    mu = jnp.mean(o, axis=-1, keepdims=True)
    var = jnp.mean(jnp.square(o - mu), axis=-1, keepdims=True)
    o = (o - mu) * lax.rsqrt(var + GN_EPS) * to_heads(lp['rw_gn_g']) + to_heads(lp['rw_gn_b'])
    o = (o + bonus[0] + bonus[1]).reshape(B, T, BR_W).astype(dt)
    return o, finals[0], finals[1]


def centred_mean(x, w):
    B, T, C = x.shape
    cs = jnp.pad(jnp.cumsum(x.astype(jnp.float32), axis=1), ((0, 0), (1, 0), (0, 0)))
    t = jnp.arange(T)
    lo = jnp.clip(t - w // 2, 0, T)
    hi = jnp.clip(t + w - w // 2, 0, T)
    cnt = (hi - lo).astype(jnp.float32)
    return (cs[:, hi] - cs[:, lo]) / cnt[None, :, None]


def pool_mix(p, w_grp, scale):
    B, T, _ = p.shape
    groups = jnp.split(p, len(POOL_WINDOWS), axis=-1)
    d = jnp.stack([centred_mean(g, w) - g.astype(jnp.float32) for g, w in zip(groups, POOL_WINDOWS)], axis=2)
    y = jnp.einsum('btgc,gcd->btgd', d.astype(p.dtype), w_grp).reshape(B, T, BR_W)
    return y * scale


def merge_branches(branches, merge_logits, w_br, w_out):
    gates = jax.nn.sigmoid(merge_logits.astype(jnp.float32)).astype(merge_logits.dtype)
    y = gates[..., :D_MODEL] * (branches[0] @ w_br[0])
    for i in range(1, N_BRANCH):
        y = y + gates[..., i * D_MODEL:(i + 1) * D_MODEL] * (branches[i] @ w_br[i])
    return y @ w_out


def branch_outputs_local(zs, lp):
    gm = gmlp_mix(zs['gm_u'], zs['gm_v'], lp['gm_ln_g'], lp['gm_ln_b'], lp['gm_ws'], lp['gm_bs']) * jax.nn.silu(zs['gm_g'])
    pl = pool_mix(zs['pl_p'], lp['pl_w'], lp['pl_scale']) * jax.nn.silu(zs['pl_g'])
    return gm, pl


def hybrid_layer(x, ctx, c, c_ctx, lp, update_ctx):
    B = x.shape[0]
    mod = jax.nn.silu(c) @ lp['ada_w'] + lp['ada_b']
    shift, scale, gate = jnp.split(mod[:, None, :], 3, axis=-1)
    mod_c = jax.nn.silu(c_ctx) @ lp['ada_w'] + lp['ada_b']
    shift_c, scale_c, gate_c = jnp.split(mod_c, 3, axis=-1)
    h = rmsnorm(x, lp['norm_g']) * (1 + scale) + shift
    hc = rmsnorm(ctx, lp['norm_g']) * (1 + scale_c) + shift_c
    zs = split_cols(h @ lp['w_in'])
    zc = split_cols(hc @ lp['w_in'])

    k_na_c, v_na_c = to_heads(zc['na_k']), to_heads(zc['na_v'])
    s0 = jnp.zeros((B, N_HEADS, HEAD_DIM, HEAD_DIM), jnp.float32)
    rw_c, s_f, s_b = rwkv_branch(zc['rw_in'], s0, s0, lp)

    gm, pl = branch_outputs_local(zs, lp)
    na = natten_latent(to_heads(zs['na_q']), to_heads(zs['na_k']), to_heads(zs['na_v']),
                       k_na_c, v_na_c, lp['na_rpb']) * jax.nn.silu(zs['na_g'])
    rw = rwkv_branch(zs['rw_in'], s_f, s_b, lp)[0] * jax.nn.silu(zs['rw_g'])
    x = x + gate * merge_branches([gm, na, rw, pl], zs['merge'], lp['w_br'], lp['w_out'])

    if update_ctx:
        gm_c, pl_c = branch_outputs_local(zc, lp)
        na_c = ctx_attention(to_heads(zc['na_q']), k_na_c, v_na_c) * jax.nn.silu(zc['na_g'])
        rw_cg = rw_c * jax.nn.silu(zc['rw_g'])
        ctx = ctx + gate_c * merge_branches([gm_c, na_c, rw_cg, pl_c], zc['merge'], lp['w_br'], lp['w_out'])
    return x, ctx


def setup_inputs(seed: int = 0) -> dict:
    key = jax.random.key(seed)
    ks = jax.random.split(key, 32)
    f32 = jnp.float32

    def nrm(k, shape, s):
        return jax.random.normal(k, shape, f32) * s

    L, D, H, N = DEPTH, D_MODEL, N_HEADS, HEAD_DIM
    return {
        "x": nrm(ks[0], (BATCH, SEQ, D), 1.0),
        "c": nrm(ks[1], (BATCH, D), 1.0),
        "ctx": nrm(ks[2], (BATCH, CTX_LEN, D), 1.0),
        "c_ctx": nrm(ks[3], (D,), 1.0),
        "ada_w": nrm(ks[4], (L, D, 3 * D), 0.5 * D ** -0.5),
        "ada_b": nrm(ks[5], (L, 3 * D), 0.02),
        "norm_g": 1.0 + nrm(ks[6], (L, D), 0.05),
        "w_in": nrm(ks[7], (L, D, IN_COLS), D ** -0.5),
        "gm_ln_g": 1.0 + nrm(ks[8], (L, H, N), 0.05),
        "gm_ln_b": nrm(ks[9], (L, H, N), 0.02),
        "gm_ws": nrm(ks[10], (L, H, CHUNK, CHUNK), CHUNK ** -0.5),
        "gm_bs": 1.0 + nrm(ks[11], (L, H, CHUNK), 0.05),
        "na_rpb": nrm(ks[12], (L, H, 2 * NA_ROWS - 1, 2 * NA_COLS - 1), 0.1),
        "rw_mu": jax.random.uniform(ks[13], (L, RW_IN), f32, 0.0, 1.0),
        "rw_w0": jax.random.uniform(ks[14], (L, 2, BR_W), f32, -6.0, -1.0),
        "rw_w2": nrm(ks[15], (L, 2, LORA_W, BR_W), 0.1),
        "rw_a0": nrm(ks[16], (L, 2, BR_W), 0.1),
        "rw_a2": nrm(ks[17], (L, 2, LORA_A, BR_W), 0.1),
        "rw_kk": 0.85 + nrm(ks[18], (L, BR_W), 0.05),
        "rw_ka": 1.0 + nrm(ks[19], (L, BR_W), 0.05),
        "rw_rk": nrm(ks[20], (L, H, N), 0.1),
        "rw_gn_g": 1.0 + nrm(ks[21], (L, BR_W), 0.05),
        "rw_gn_b": nrm(ks[22], (L, BR_W), 0.02),
        "pl_w": nrm(ks[23], (L, H, N, N), N ** -0.5),
        "pl_scale": 1.0 + nrm(ks[24], (L, BR_W), 0.1),
        "w_br": nrm(ks[25], (L, N_BRANCH, BR_W, D), BR_W ** -0.5),
        "w_out": nrm(ks[26], (L, D, D), D ** -0.5),
        "final_g": 1.0 + nrm(ks[27], (D,), 0.05),
    }


def reference(x, c, ctx, c_ctx, ada_w, ada_b, norm_g, w_in, gm_ln_g, gm_ln_b, gm_ws, gm_bs, na_rpb,
              rw_mu, rw_w0, rw_w2, rw_a0, rw_a2, rw_kk, rw_ka, rw_rk, rw_gn_g, rw_gn_b,
              pl_w, pl_scale, w_br, w_out, final_g):
    for l in range(DEPTH):
        lp = {
            'ada_w': ada_w[l], 'ada_b': ada_b[l], 'norm_g': norm_g[l], 'w_in': w_in[l],
            'gm_ln_g': gm_ln_g[l], 'gm_ln_b': gm_ln_b[l], 'gm_ws': gm_ws[l], 'gm_bs': gm_bs[l],
            'na_rpb': na_rpb[l],
            'rw_mu': rw_mu[l], 'rw_w0': rw_w0[l], 'rw_w2': rw_w2[l], 'rw_a0': rw_a0[l], 'rw_a2': rw_a2[l],
            'rw_kk': rw_kk[l], 'rw_ka': rw_ka[l], 'rw_rk': rw_rk[l], 'rw_gn_g': rw_gn_g[l], 'rw_gn_b': rw_gn_b[l],
            'pl_w': pl_w[l], 'pl_scale': pl_scale[l], 'w_br': w_br[l], 'w_out': w_out[l],
        }
        x, ctx = hybrid_layer(x, ctx, c, c_ctx, lp, update_ctx=(l < DEPTH - 1))
    return rmsnorm(x, final_g)
```

```python
import functools

import jax
import jax.numpy as jnp
from jax import lax
from jax.experimental import pallas as pl
from jax.experimental.pallas import tpu as pltpu

F32 = jnp.float32
BF16 = jnp.bfloat16

N_HEADS = 4
HEAD_DIM = 64
BR_W = N_HEADS * HEAD_DIM
LANES = 128
GRID_W = 64
NA_ROWS = 8
NA_COLS = 16
CHUNK = 128
LORA = 64
POOL_WINDOWS = (2, 4, 8, 16)
POOL_HALO = 16
SCAN_L = 64
EPS = 1e-6
GN_EPS = 64e-5
NEG = -1e30
VMEM_LIMIT = 56 * 1024 * 1024

SEG_GM, SEG_NA, SEG_RWIN, SEG_RWG, SEG_PL = 3 * BR_W, 4 * BR_W, 3 * BR_W + 2 * LORA, BR_W, 2 * BR_W
MIX_COLS = SEG_GM + SEG_NA + SEG_RWIN + SEG_RWG + SEG_PL


def _params(*sem):
    return pltpu.CompilerParams(dimension_semantics=sem, vmem_limit_bytes=VMEM_LIMIT)


def _sigmoid(x):
    return 1.0 / (1.0 + jnp.exp(-x))


def _silu(x):
    return x * _sigmoid(x)


def _gelu(x):
    return 0.5 * x * (1.0 + jnp.tanh(0.7978845608028654 * (x + 0.044715 * x * x * x)))


def _mm(a, b):
    return jnp.dot(a.astype(BF16), b.astype(BF16), preferred_element_type=F32)


def _mm_nt(a, b):
    return lax.dot_general(a.astype(BF16), b.astype(BF16), (((1,), (1,)), ((), ())),
                           preferred_element_type=F32)


def _mm_tn(a, b):
    return lax.dot_general(a.astype(BF16), b.astype(BF16), (((0,), (0,)), ((), ())),
                           preferred_element_type=F32)


def _split(x):
    hi = x.astype(BF16)
    lo = (x - hi.astype(F32)).astype(BF16)
    return hi, lo


def _mm3(a, b):
    ah, al = _split(a)
    bh, bl = _split(b)
    d = functools.partial(jnp.dot, preferred_element_type=F32)
    return d(ah, bh) + d(ah, bl) + d(al, bh)


def _mm_exact_rhs(a, b_bf16):
    d = functools.partial(jnp.dot, preferred_element_type=F32)
    a0 = a.astype(BF16)
    r1 = a - a0.astype(F32)
    a1 = r1.astype(BF16)
    a2 = (r1 - a1.astype(F32)).astype(BF16)
    return d(a0, b_bf16) + d(a1, b_bf16) + d(a2, b_bf16)


def _mm_exact_lhs(a_bf16, b):
    d = functools.partial(jnp.dot, preferred_element_type=F32)
    b0 = b.astype(BF16)
    r1 = b - b0.astype(F32)
    b1 = r1.astype(BF16)
    b2 = (r1 - b1.astype(F32)).astype(BF16)
    return d(a_bf16, b0) + d(a_bf16, b1) + d(a_bf16, b2)


def _group_matrix(value):
    i = lax.broadcasted_iota(jnp.int32, (BR_W, BR_W), 0) // HEAD_DIM
    j = lax.broadcasted_iota(jnp.int32, (BR_W, BR_W), 1) // HEAD_DIM
    return jnp.where(i == j, value, 0.0).astype(BF16)


def _head_of_lane(shape):
    return lax.broadcasted_iota(jnp.int32, shape, len(shape) - 1) // HEAD_DIM


def _norm_mod(x, g, shift, scale):
    ms = jnp.mean(x * x, axis=-1, keepdims=True)
    return (x * lax.rsqrt(ms + EPS) * g) * (1.0 + scale) + shift


def _mod_kernel(cs_ref, w_ref, b_ref, o_ref):
    o_ref[0] = _mm3(_silu(cs_ref[...]), w_ref[0]) + b_ref[0]


def _modulation(cs, ada_w, ada_b):
    depth, d, d3 = ada_w.shape
    rows = cs.shape[0]
    tn = 768
    return pl.pallas_call(
        _mod_kernel,
        grid=(depth, d3 // tn),
        in_specs=[pl.BlockSpec((rows, d), lambda l, j: (0, 0)),
                  pl.BlockSpec((1, d, tn), lambda l, j: (l, 0, j)),
                  pl.BlockSpec((1, 1, tn), lambda l, j: (l, 0, j))],
        out_specs=pl.BlockSpec((1, rows, tn), lambda l, j: (l, 0, j)),
        out_shape=jax.ShapeDtypeStruct((depth, rows, d3), F32),
        compiler_params=_params("arbitrary", "arbitrary"),
        name="adaln_modulation",
    )(cs, ada_w, ada_b.reshape(depth, 1, d3))


def _proj_kernel(x_ref, g_ref, sh_ref, sc_ref, w_ref, gm_ref, na_ref, rwin_ref, rwg_ref, pl_ref):
    h = _norm_mod(x_ref[0], g_ref[...], sh_ref[0], sc_ref[0]).astype(BF16)
    off = 0
    for ref in (gm_ref, na_ref, rwin_ref, rwg_ref, pl_ref):
        width = ref.shape[-1]
        ref[0] = jnp.dot(h, w_ref[:, off:off + width], preferred_element_type=F32)
        off += width


def _project(x, norm_g, shift, scale, w_mix):
    b, t, d = x.shape
    tm = min(t, 256)
    widths = (SEG_GM, SEG_NA, SEG_RWIN, SEG_RWG, SEG_PL)
    row = lambda i, j: (i, j, 0)
    per_b = lambda i, j: (i, 0, 0)
    return pl.pallas_call(
        _proj_kernel,
        grid=(b, t // tm),
        in_specs=[pl.BlockSpec((1, tm, d), row),
                  pl.BlockSpec((1, d), lambda i, j: (0, 0)),
                  pl.BlockSpec((1, 1, d), per_b),
                  pl.BlockSpec((1, 1, d), per_b),
                  pl.BlockSpec((d, MIX_COLS), lambda i, j: (0, 0))],
        out_specs=[pl.BlockSpec((1, tm, w), row) for w in widths],
        out_shape=[jax.ShapeDtypeStruct((b, t, w), F32) for w in widths],
        compiler_params=_params("parallel", "parallel"),
        name="norm_in_proj",
    )(x, norm_g.reshape(1, d), shift, scale, w_mix)


def _gmlp_kernel(z_ref, lng_ref, lnb_ref, ws_ref, bs_ref, o_ref):
    z = z_ref[0]
    tm = z.shape[0]
    u = _gelu(z[:, 0:BR_W])
    v = _gelu(z[:, BR_W:2 * BR_W])
    gate = z[:, 2 * BR_W:3 * BR_W]
    avg = _group_matrix(1.0 / HEAD_DIM)
    mu = _mm_exact_rhs(v, avg)
    vc = v - mu
    var = _mm_exact_rhs(vc * vc, avg)
    vh = (vc * lax.rsqrt(var + EPS) * lng_ref[...] + lnb_ref[...]).astype(BF16)
    head = _head_of_lane((CHUNK, BR_W))
    su = u * _silu(gate)
    for c in range(tm // CHUNK):
        rows = slice(c * CHUNK, (c + 1) * CHUNK)
        vch = vh[rows]
        mixed = jnp.zeros((CHUNK, BR_W), F32)
        for g in range(N_HEADS):
            mg = jnp.dot(ws_ref[g], vch, preferred_element_type=F32)
            mixed = jnp.where(head == g, mg, mixed)
        o_ref[0, rows, :] = su[rows] * (mixed + bs_ref[...])


def _gmlp(gm, ln_g, ln_b, w_s, b_s):
    b, t, _ = gm.shape
    tm = min(t, 512)
    bs_full = jnp.repeat(b_s.T, HEAD_DIM, axis=1)
    const2 = lambda i, j: (0, 0)
    return pl.pallas_call(
        _gmlp_kernel,
        grid=(b, t // tm),
        in_specs=[pl.BlockSpec((1, tm, SEG_GM), lambda i, j: (i, j, 0)),
                  pl.BlockSpec((1, BR_W), const2),
                  pl.BlockSpec((1, BR_W), const2),
                  pl.BlockSpec((N_HEADS, CHUNK, CHUNK), lambda i, j: (0, 0, 0)),
                  pl.BlockSpec((CHUNK, BR_W), const2)],
        out_specs=pl.BlockSpec((1, tm, BR_W), lambda i, j: (i, j, 0)),
        out_shape=jax.ShapeDtypeStruct((b, t, BR_W), F32),
        compiler_params=_params("parallel", "parallel"),
        name="gmlp_mixer",
    )(gm, ln_g.reshape(1, BR_W), ln_b.reshape(1, BR_W), w_s.astype(BF16), bs_full)


def _pool_kernel(p_ref, g_ref, w_ref, sc_ref, o_ref, ext_ref, *, seq_len):
    j = pl.program_id(1)
    tm = o_ref.shape[1]
    n = tm + 2 * POOL_HALO

    @pl.when(j == 0)
    def _():
        ext_ref[0:POOL_HALO, :] = jnp.zeros((POOL_HALO, BR_W), F32)
        ext_ref[POOL_HALO + seq_len:, :] = jnp.zeros((POOL_HALO, BR_W), F32)
        ext_ref[POOL_HALO:POOL_HALO + seq_len, :] = p_ref[0]

    t0 = pl.multiple_of(j * tm, 8)
    x = ext_ref[pl.ds(t0, n), :]
    s2 = x + pltpu.roll(x, 1, 0)
    s4 = pltpu.roll(s2, 1, 0) + pltpu.roll(s2, n - 1, 0)
    s8 = pltpu.roll(s4, 2, 0) + pltpu.roll(s4, n - 2, 0)
    s16 = pltpu.roll(s8, 4, 0) + pltpu.roll(s8, n - 4, 0)
    mid = slice(POOL_HALO, POOL_HALO + tm)
    grp = _head_of_lane((tm, BR_W))
    ssum = jnp.where(grp == 0, s2[mid], jnp.where(grp == 1, s4[mid], jnp.where(grp == 2, s8[mid], s16[mid])))
    half = jnp.where(grp == 0, 1, jnp.where(grp == 1, 2, jnp.where(grp == 2, 4, 8)))
    t = t0 + lax.broadcasted_iota(jnp.int32, (tm, BR_W), 0)
    cnt = jnp.minimum(t + half, seq_len) - jnp.maximum(t - half, 0)
    d = ssum / cnt.astype(F32) - x[mid]
    y = _mm(d, w_ref[...])
    o_ref[0] = y * sc_ref[...] * _silu(g_ref[0])


def _pool(plz, pl_w, pl_scale):
    b, t, _ = plz.shape
    tm = min(t, 512)
    w_bd = jax.scipy.linalg.block_diag(*[pl_w[g] for g in range(N_HEADS)]).astype(BF16)
    return pl.pallas_call(
        functools.partial(_pool_kernel, seq_len=t),
        grid=(b, t // tm),
        in_specs=[pl.BlockSpec((1, t, BR_W), lambda i, j: (i, 0, 0)),
                  pl.BlockSpec((1, tm, BR_W), lambda i, j: (i, j, 1)),
                  pl.BlockSpec((BR_W, BR_W), lambda i, j: (0, 0)),
                  pl.BlockSpec((1, BR_W), lambda i, j: (0, 0))],
        out_specs=pl.BlockSpec((1, tm, BR_W), lambda i, j: (i, j, 0)),
        out_shape=jax.ShapeDtypeStruct((b, t, BR_W), F32),
        scratch_shapes=[pltpu.VMEM((t + 2 * POOL_HALO, BR_W), F32)],
        compiler_params=_params("parallel", "arbitrary"),
        name="pool_mixer",
    )(plz, plz, w_bd, pl_scale.reshape(1, BR_W))


def _stack_heads(q):
    head = _head_of_lane(q.shape)
    return jnp.concatenate([jnp.where(head == h, q, 0.0) for h in range(N_HEADS)], axis=0)


def _unstack_heads(o, n):
    head = _head_of_lane((n, BR_W))
    out = jnp.zeros((n, BR_W), F32)
    for h in range(N_HEADS):
        out = jnp.where(head == h, o[h * n:(h + 1) * n], out)
    return out


def _natten_kernel(q_ref, k_ref, v_ref, g_ref, kc_ref, vc_ref, bias_ref, o_ref, *, rows):
    r = pl.program_id(1)
    band = NA_ROWS * GRID_W
    rs = jnp.clip(r - NA_ROWS // 2, 0, rows - NA_ROWS)
    start = pl.multiple_of(rs * GRID_W, GRID_W)
    qs = _stack_heads(q_ref[0] * (HEAD_DIM ** -0.5)).astype(BF16)
    kb = k_ref[0, pl.ds(start, band), :].astype(BF16)
    vb = v_ref[0, pl.ds(start, band), :].astype(BF16)
    s_b = _mm_nt(qs, kb) + bias_ref[rs - r + NA_ROWS - 1]
    s_c = _mm_nt(qs, kc_ref[0])
    m = jnp.maximum(jnp.max(s_b, axis=1, keepdims=True), jnp.max(s_c, axis=1, keepdims=True))
    e_b = jnp.exp(s_b - m)
    e_c = jnp.exp(s_c - m)
    den = jnp.sum(e_b, axis=1, keepdims=True) + jnp.sum(e_c, axis=1, keepdims=True)
    pv = (_mm(e_b, vb) + _mm(e_c, vc_ref[0])) / den
    o_ref[0] = _unstack_heads(pv, GRID_W) * _silu(g_ref[0])


def _natten_bias(rpb):
    c = jnp.arange(GRID_W)
    c_start = jnp.clip(c - NA_COLS // 2, 0, GRID_W - NA_COLS)
    col_ok = (c[None, :] >= c_start[:, None]) & (c[None, :] < c_start[:, None] + NA_COLS)
    dc = jnp.clip(c[None, :] - c[:, None], -(NA_COLS - 1), NA_COLS - 1) + NA_COLS - 1
    dr = jnp.arange(NA_ROWS)[:, None] + jnp.arange(NA_ROWS)[None, :]
    tab = rpb[:, dr][..., dc]
    tab = jnp.where(col_ok[None, None, None], tab, NEG)
    tab = tab.transpose(1, 0, 3, 2, 4)
    return tab.reshape(NA_ROWS, N_HEADS * GRID_W, NA_ROWS * GRID_W)


def _natten(na, na_ctx, rpb):
    b, t, _ = na.shape
    lc = na_ctx.shape[1]
    rows = t // GRID_W
    band = NA_ROWS * GRID_W
    qrow = lambda c: (lambda i, r: (i, r, c))
    full = lambda c: (lambda i, r: (i, 0, c))
    return pl.pallas_call(
        functools.partial(_natten_kernel, rows=rows),
        grid=(b, rows),
        in_specs=[pl.BlockSpec((1, GRID_W, BR_W), qrow(0)),
                  pl.BlockSpec((1, t, BR_W), full(1)),
                  pl.BlockSpec((1, t, BR_W), full(2)),
                  pl.BlockSpec((1, GRID_W, BR_W), qrow(3)),
                  pl.BlockSpec((1, lc, BR_W), full(1)),
                  pl.BlockSpec((1, lc, BR_W), full(2)),
                  pl.BlockSpec((NA_ROWS, N_HEADS * GRID_W, band), lambda i, r: (0, 0, 0))],
        out_specs=pl.BlockSpec((1, GRID_W, BR_W), qrow(0)),
        out_shape=jax.ShapeDtypeStruct((b, t, BR_W), F32),
        compiler_params=_params("parallel", "arbitrary"),
        name="natten_mixer",
    )(na, na, na, na, na_ctx, na_ctx, _natten_bias(rpb))


def _ctx_attn_kernel(q_ref, k_ref, v_ref, g_ref, o_ref):
    n = q_ref.shape[1]
    qs = _stack_heads(q_ref[0] * (HEAD_DIM ** -0.5))
    s = _mm_nt(qs, k_ref[0])
    e = jnp.exp(s - jnp.max(s, axis=1, keepdims=True))
    pv = _mm(e, v_ref[0]) / jnp.sum(e, axis=1, keepdims=True)
    o_ref[0] = _unstack_heads(pv, n) * _silu(g_ref[0])


def _ctx_attention(na_ctx):
    b, lc, _ = na_ctx.shape
    tq = 64
    qrow = lambda c: (lambda i, j: (i, j, c))
    full = lambda c: (lambda i, j: (i, 0, c))
    return pl.pallas_call(
        _ctx_attn_kernel,
        grid=(b, lc // tq),
        in_specs=[pl.BlockSpec((1, tq, BR_W), qrow(0)),
                  pl.BlockSpec((1, lc, BR_W), full(1)),
                  pl.BlockSpec((1, lc, BR_W), full(2)),
                  pl.BlockSpec((1, tq, BR_W), qrow(3))],
        out_specs=pl.BlockSpec((1, tq, BR_W), qrow(0)),
        out_shape=jax.ShapeDtypeStruct((b, lc, BR_W), F32),
        compiler_params=_params("parallel", "arbitrary"),
        name="ctx_attention",
    )(na_ctx, na_ctx, na_ctx, na_ctx)


PREP_R, PREP_V, PREP_A = 0, 1, 2
PREP_DIR = 3
PREP_BONUS = 9
PREP_COLS = 10 * BR_W


def _softplus(x):
    return jnp.maximum(x, 0.0) + jnp.log(1.0 + jnp.exp(-jnp.abs(x)))


def _prep_kernel(z_ref, zp_ref, zn_ref, mu_ref, kkw_ref, ka_ref, rk_ref, w0_ref, a0_ref, w2_ref, a2_ref, o_ref):
    j = pl.program_id(1)
    z = z_ref[0]
    tp = z.shape[0]
    row = lax.broadcasted_iota(jnp.int32, z.shape, 0)
    prev_row = jnp.where(j > 0, zp_ref[0, 7:8, :], 0.0)
    next_row = jnp.where(j < pl.num_programs(1) - 1, zn_ref[0, 0:1, :], 0.0)
    prev = jnp.where(row == 0, prev_row, pltpu.roll(z, 1, 0))
    nxt = jnp.where(row == tp - 1, next_row, pltpu.roll(z, tp - 1, 0))
    zf = z + (0.5 * (prev + nxt) - z) * mu_ref[...]
    r = zf[:, 0:BR_W]
    k = zf[:, BR_W:2 * BR_W]
    v = zf[:, 2 * BR_W:3 * BR_W]
    lo = zf[:, 3 * BR_W:3 * BR_W + 2 * LORA]
    ones = _group_matrix(1.0)
    kk = k * kkw_ref[...]
    kk = kk / jnp.maximum(jnp.sqrt(_mm_exact_rhs(kk * kk, ones)), 1e-12)
    o_ref[0, :, PREP_R * BR_W:(PREP_R + 1) * BR_W] = r
    o_ref[0, :, PREP_V * BR_W:(PREP_V + 1) * BR_W] = v
    o_ref[0, :, PREP_A * BR_W:(PREP_A + 1) * BR_W] = -kk
    tlo = jnp.tanh(lo)
    kd_sum = jnp.zeros_like(k)
    for d in range(2):
        w = -_softplus(-(w0_ref[d:d + 1, :] + _mm3(tlo, w2_ref[d]))) - 0.5
        a = _sigmoid(a0_ref[d:d + 1, :] + _mm3(lo, a2_ref[d]))
        kd = k * (1.0 + (a - 1.0) * ka_ref[...])
        c0 = (PREP_DIR + 3 * d) * BR_W
        o_ref[0, :, c0:c0 + BR_W] = -jnp.exp(w)
        o_ref[0, :, c0 + BR_W:c0 + 2 * BR_W] = kd
        o_ref[0, :, c0 + 2 * BR_W:c0 + 3 * BR_W] = kk * a
        kd_sum = kd_sum + kd
    bonus = _mm_exact_rhs(r * kd_sum * rk_ref[...], ones) * v
    o_ref[0, :, PREP_BONUS * BR_W:] = bonus


def _rwkv_prep(rwin, lp):
    b, t, width = rwin.shape
    tp = min(t, 256)
    nblk8 = t // 8
    zeros = jnp.zeros((2, LORA, BR_W), F32)
    w2 = jnp.concatenate([lp['rw_w2'], zeros], axis=1)
    a2 = jnp.concatenate([zeros, lp['rw_a2']], axis=1)
    vec = lambda i, j: (0, 0)
    return pl.pallas_call(
        _prep_kernel,
        grid=(b, t // tp),
        in_specs=[pl.BlockSpec((1, tp, width), lambda i, j: (i, j, 0)),
                  pl.BlockSpec((1, 8, width), lambda i, j: (i, jnp.maximum(j * (tp // 8) - 1, 0), 0)),
                  pl.BlockSpec((1, 8, width), lambda i, j: (i, jnp.minimum((j + 1) * (tp // 8), nblk8 - 1), 0)),
                  pl.BlockSpec((1, width), vec),
                  pl.BlockSpec((1, BR_W), vec),
                  pl.BlockSpec((1, BR_W), vec),
                  pl.BlockSpec((1, BR_W), vec),
                  pl.BlockSpec((2, BR_W), vec),
                  pl.BlockSpec((2, BR_W), vec),
                  pl.BlockSpec((2, 2 * LORA, BR_W), lambda i, j: (0, 0, 0)),
                  pl.BlockSpec((2, 2 * LORA, BR_W), lambda i, j: (0, 0, 0))],
        out_specs=pl.BlockSpec((1, tp, PREP_COLS), lambda i, j: (i, j, 0)),
        out_shape=jax.ShapeDtypeStruct((b, t, PREP_COLS), F32),
        compiler_params=_params("parallel", "parallel"),
        name="rwkv_prep",
    )(rwin, rwin, rwin, lp['rw_mu'].reshape(1, width), lp['rw_kk'].reshape(1, BR_W),
      lp['rw_ka'].reshape(1, BR_W), lp['rw_rk'].reshape(1, BR_W), lp['rw_w0'], lp['rw_a0'], w2, a2)


def _stack_pair(x):
    head = _head_of_lane(x.shape)
    return jnp.concatenate([jnp.where(head == 0, x, 0.0), jnp.where(head == 1, x, 0.0)], axis=0)


def _scan_chunk(r, v, a, lw, kd, bv, state, reverse):
    n = r.shape[0]
    ti = lax.broadcasted_iota(jnp.int32, (n, n), 0)
    si = lax.broadcasted_iota(jnp.int32, (n, n), 1)
    earlier_or_same = (si >= ti) if reverse else (si <= ti)
    cum = _mm_exact_lhs(jnp.where(earlier_or_same, 1.0, 0.0).astype(BF16), lw)
    tot = jnp.sum(lw, axis=0, keepdims=True)
    e_neg = jnp.exp(-cum)
    e_rest = jnp.exp(tot - cum)
    a_st = _stack_pair(a * jnp.exp(cum - lw))
    r_st = _stack_pair(r * jnp.exp(cum))
    b_st = _stack_pair(bv * e_neg)
    k_st = _stack_pair(kd * e_neg)
    v_st = _stack_pair(v)

    i2 = lax.broadcasted_iota(jnp.int32, (2 * n, 2 * n), 0) % n
    j2 = lax.broadcasted_iota(jnp.int32, (2 * n, 2 * n), 1) % n
    strict = (j2 > i2) if reverse else (j2 < i2)
    incl = (j2 >= i2) if reverse else (j2 <= i2)
    a_ab = jnp.where(strict, _mm_nt(a_st, b_st), 0.0)
    a_ak = jnp.where(strict, _mm_nt(a_st, k_st), 0.0)
    a_rb = jnp.where(incl, _mm_nt(r_st, b_st), 0.0)
    a_rk = jnp.where(incl, _mm_nt(r_st, k_st), 0.0)

    u = _mm_nt(a_st, state) + _mm(a_ak, v_st)
    p = a_ab
    step = 1
    while step < n:
        u = u + _mm3(p, u)
        step *= 2
        if step < n:
            p = _mm3(p, p)

    y_st = _mm_nt(r_st, state) + _mm(a_rb, u) + _mm(a_rk, v_st)
    y = y_st[0:n] + y_st[n:2 * n]
    new_state = (state * jnp.exp(tot) + _mm_tn(u, _stack_pair(bv * e_rest))
                 + _mm_tn(v_st, _stack_pair(kd * e_rest)))
    return y, new_state


def _scan_kernel(sf_ref, df_ref, sb_ref, db_ref, s0_ref, yf_ref, yb_ref, sfin_ref, state_ref):
    j = pl.program_id(1)

    @pl.when(j == 0)
    def _():
        state_ref[...] = s0_ref[0]

    for d, (s_ref, d_ref, y_ref) in enumerate(((sf_ref, df_ref, yf_ref), (sb_ref, db_ref, yb_ref))):
        for p in range(BR_W // LANES):
            col = lambda blk: slice(blk * BR_W + p * LANES, blk * BR_W + (p + 1) * LANES)
            y, new_state = _scan_chunk(
                s_ref[0, :, col(PREP_R)], s_ref[0, :, col(PREP_V)], s_ref[0, :, col(PREP_A)],
                d_ref[0, :, col(0)], d_ref[0, :, col(1)], d_ref[0, :, col(2)],
                state_ref[d, p], reverse=(d == 1))
            y_ref[0, :, p * LANES:(p + 1) * LANES] = y
            state_ref[d, p] = new_state

    @pl.when(j == pl.num_programs(1) - 1)
    def _():
        sfin_ref[0] = state_ref[...]


def _rwkv_scan(prep, s0):
    b, t, _ = prep.shape
    n = t // SCAN_L
    npair = BR_W // LANES
    blk3 = 3 * BR_W
    fwd = lambda c: (lambda i, j: (i, j, c))
    bwd = lambda c: (lambda i, j: (i, n - 1 - j, c))
    st = lambda i, j: (i, 0, 0, 0, 0)
    return pl.pallas_call(
        _scan_kernel,
        grid=(b, n),
        in_specs=[pl.BlockSpec((1, SCAN_L, blk3), fwd(0)),
                  pl.BlockSpec((1, SCAN_L, blk3), fwd(1)),
                  pl.BlockSpec((1, SCAN_L, blk3), bwd(0)),
                  pl.BlockSpec((1, SCAN_L, blk3), bwd(2)),
                  pl.BlockSpec((1, 2, npair, LANES, LANES), st)],
        out_specs=[pl.BlockSpec((1, SCAN_L, BR_W), fwd(0)),
                   pl.BlockSpec((1, SCAN_L, BR_W), bwd(0)),
                   pl.BlockSpec((1, 2, npair, LANES, LANES), st)],
        out_shape=[jax.ShapeDtypeStruct((b, t, BR_W), F32),
                   jax.ShapeDtypeStruct((b, t, BR_W), F32),
                   jax.ShapeDtypeStruct((b, 2, npair, LANES, LANES), F32)],
        scratch_shapes=[pltpu.VMEM((2, npair, LANES, LANES), F32)],
        compiler_params=_params("parallel", "arbitrary"),
        name="rwkv_scan",
    )(prep, prep, prep, prep, s0)


def _rwkv_out_kernel(yf_ref, yb_ref, bonus_ref, g_ref, gng_ref, gnb_ref, o_ref):
    o = yf_ref[0] + yb_ref[0]
    avg = _group_matrix(1.0 / HEAD_DIM)
    mu = _mm_exact_rhs(o, avg)
    oc = o - mu
    var = _mm_exact_rhs(oc * oc, avg)
    o = oc * lax.rsqrt(var + GN_EPS) * gng_ref[...] + gnb_ref[...] + bonus_ref[0]
    o_ref[0] = o * _silu(g_ref[0])


def _rwkv_out(yf, yb, prep, rwg, gn_g, gn_b):
    b, t, _ = yf.shape
    tm = min(t, 512)
    row = lambda i, j: (i, j, 0)
    vec = lambda i, j: (0, 0)
    return pl.pallas_call(
        _rwkv_out_kernel,
        grid=(b, t // tm),
        in_specs=[pl.BlockSpec((1, tm, BR_W), row),
                  pl.BlockSpec((1, tm, BR_W), row),
                  pl.BlockSpec((1, tm, BR_W), lambda i, j: (i, j, PREP_BONUS)),
                  pl.BlockSpec((1, tm, BR_W), row),
                  pl.BlockSpec((1, BR_W), vec),
                  pl.BlockSpec((1, BR_W), vec)],
        out_specs=pl.BlockSpec((1, tm, BR_W), row),
        out_shape=jax.ShapeDtypeStruct((b, t, BR_W), F32),
        compiler_params=_params("parallel", "parallel"),
        name="rwkv_groupnorm_gate",
    )(yf, yb, prep, rwg, gn_g.reshape(1, BR_W), gn_b.reshape(1, BR_W))


def _merge_kernel(x_ref, g_ref, sh_ref, sc_ref, gt_ref, gm_ref, na_ref, rw_ref, pl_ref,
                  wm_ref, wbr_ref, wout_ref, fin_ref, o_ref, *, final):
    x = x_ref[0]
    d = x.shape[-1]
    h = _norm_mod(x, g_ref[...], sh_ref[0], sc_ref[0]).astype(BF16)
    y = None
    for i, br_ref in enumerate((gm_ref, na_ref, rw_ref, pl_ref)):
        logits = jnp.dot(h, wm_ref[:, i * d:(i + 1) * d], preferred_element_type=F32)
        term = _sigmoid(logits) * jnp.dot(br_ref[0].astype(BF16), wbr_ref[i], preferred_element_type=F32)
        y = term if y is None else y + term
    out = x + gt_ref[0] * jnp.dot(y.astype(BF16), wout_ref[...], preferred_element_type=F32)
    if final:
        ms = jnp.mean(out * out, axis=-1, keepdims=True)
        out = out * lax.rsqrt(ms + EPS) * fin_ref[...]
    o_ref[0] = out


def _merge(x, norm_g, shift, scale, gate, branches, w_merge, w_br, w_out, final_g, final):
    b, t, d = x.shape
    tm = min(t, 256)
    row = lambda i, j: (i, j, 0)
    per_b = lambda i, j: (i, 0, 0)
    vec = lambda i, j: (0, 0)
    return pl.pallas_call(
        functools.partial(_merge_kernel, final=final),
        grid=(b, t // tm),
        in_specs=[pl.BlockSpec((1, tm, d), row),
                  pl.BlockSpec((1, d), vec),
                  pl.BlockSpec((1, 1, d), per_b),
                  pl.BlockSpec((1, 1, d), per_b),
                  pl.BlockSpec((1, 1, d), per_b)]
                 + [pl.BlockSpec((1, tm, BR_W), row)] * 4
                 + [pl.BlockSpec(w_merge.shape, vec),
                    pl.BlockSpec(w_br.shape, lambda i, j: (0, 0, 0)),
                    pl.BlockSpec(w_out.shape, vec),
                    pl.BlockSpec((1, d), vec)],
        out_specs=pl.BlockSpec((1, tm, d), row),
        out_shape=jax.ShapeDtypeStruct((b, t, d), F32),
        compiler_params=_params("parallel", "parallel"),
        name="merge_out_proj",
    )(x, norm_g.reshape(1, d), shift, scale, gate, *branches, w_merge, w_br, w_out, final_g.reshape(1, d))


def _rwkv_branch(rwin, rwg, s0, lp):
    prep = _rwkv_prep(rwin, lp)
    yf, yb, s_fin = _rwkv_scan(prep, s0)
    return functools.partial(_rwkv_out, yf, yb, prep, rwg, lp['rw_gn_g'], lp['rw_gn_b']), s_fin


def _layer(x, ctx, mod_x, mod_c, lp, final_g, last):
    b, _, d = x.shape
    w_in = lp['w_in']
    w_mix = w_in[:, :MIX_COLS].astype(BF16)
    w_merge = w_in[:, MIX_COLS:].astype(BF16)
    w_br = lp['w_br'].astype(BF16)
    w_out = lp['w_out'].astype(BF16)

    gm_c, na_c, rwin_c, rwg_c, pl_c = _project(ctx, lp['norm_g'], mod_c[0], mod_c[1], w_mix)
    gm_x, na_x, rwin_x, rwg_x, pl_x = _project(x, lp['norm_g'], mod_x[0], mod_x[1], w_mix)

    zero_state = jnp.zeros((b, 2, BR_W // LANES, LANES, LANES), F32)
    rw_c_out, s_ctx = _rwkv_branch(rwin_c, rwg_c, zero_state, lp)
    rw_x_out, _ = _rwkv_branch(rwin_x, rwg_x, s_ctx, lp)

    branches = [_gmlp(gm_x, lp['gm_ln_g'], lp['gm_ln_b'], lp['gm_ws'], lp['gm_bs']),
                _natten(na_x, na_c, lp['na_rpb']),
                rw_x_out(),
                _pool(pl_x, lp['pl_w'], lp['pl_scale'])]
    x_new = _merge(x, lp['norm_g'], mod_x[0], mod_x[1], mod_x[2], branches,
                   w_merge, w_br, w_out, final_g, final=last)
    if last:
        return x_new, ctx
    branches_c = [_gmlp(gm_c, lp['gm_ln_g'], lp['gm_ln_b'], lp['gm_ws'], lp['gm_bs']),
                  _ctx_attention(na_c),
                  rw_c_out(),
                  _pool(pl_c, lp['pl_w'], lp['pl_scale'])]
    ctx_new = _merge(ctx, lp['norm_g'], mod_c[0], mod_c[1], mod_c[2], branches_c,
                     w_merge, w_br, w_out, final_g, final=False)
    return x_new, ctx_new


def kernel(x, c, ctx, c_ctx, ada_w, ada_b, norm_g, w_in, gm_ln_g, gm_ln_b, gm_ws, gm_bs, na_rpb, rw_mu, rw_w0, rw_w2, rw_a0, rw_a2, rw_kk, rw_ka, rw_rk, rw_gn_g, rw_gn_b, pl_w, pl_scale, w_br, w_out, final_g):
    b, _, d = x.shape
    depth = ada_w.shape[0]
    mod_rows = 16
    cs = jnp.zeros((mod_rows, d), F32).at[:b].set(c).at[b].set(c_ctx)
    mod = _modulation(cs, ada_w, ada_b)
    for l in range(depth):
        lp = {
            'norm_g': norm_g[l], 'w_in': w_in[l],
            'gm_ln_g': gm_ln_g[l], 'gm_ln_b': gm_ln_b[l], 'gm_ws': gm_ws[l], 'gm_bs': gm_bs[l],
            'na_rpb': na_rpb[l],
            'rw_mu': rw_mu[l], 'rw_w0': rw_w0[l], 'rw_w2': rw_w2[l], 'rw_a0': rw_a0[l], 'rw_a2': rw_a2[l],
            'rw_kk': rw_kk[l], 'rw_ka': rw_ka[l], 'rw_rk': rw_rk[l], 'rw_gn_g': rw_gn_g[l], 'rw_gn_b': rw_gn_b[l],
            'pl_w': pl_w[l], 'pl_scale': pl_scale[l], 'w_br': w_br[l], 'w_out': w_out[l],
        }
        mod_x = [mod[l, :b, i * d:(i + 1) * d].reshape(b, 1, d) for i in range(3)]
        mod_c = [jnp.broadcast_to(mod[l, b, i * d:(i + 1) * d].reshape(1, 1, d), (b, 1, d)) for i in range(3)]
        x, ctx = _layer(x, ctx, mod_x, mod_c, lp, final_g, last=(l == depth - 1))
    return x
```

```python
import functools

import jax
import jax.numpy as jnp
from jax import lax
from jax.experimental import pallas as pl
from jax.experimental.pallas import tpu as pltpu

F32 = jnp.float32
BF16 = jnp.bfloat16

N_HEADS = 4
HEAD_DIM = 64
BR_W = N_HEADS * HEAD_DIM
LANES = 128
GRID_W = 64
NA_ROWS = 8
NA_COLS = 16
NA_RT = 4
CHUNK = 128
LORA = 64
POOL_WINDOWS = (2, 4, 8, 16)
POOL_HALO = 16
SCAN_L = 64
SCAN_NB = 4
EPS = 1e-6
GN_EPS = 64e-5
NEG = -1e30
VMEM_LIMIT = 56 * 1024 * 1024

SEG_GM, SEG_NA, SEG_RWIN, SEG_RWG, SEG_PL = 3 * BR_W, 4 * BR_W, 3 * BR_W + 2 * LORA, BR_W, 2 * BR_W
MIX_COLS = SEG_GM + SEG_NA + SEG_RWIN + SEG_RWG + SEG_PL


def _params(*sem):
    return pltpu.CompilerParams(dimension_semantics=sem, vmem_limit_bytes=VMEM_LIMIT)


def _sigmoid(x):
    return 1.0 / (1.0 + jnp.exp(-x))


def _silu(x):
    return x * _sigmoid(x)


def _gelu(x):
    return 0.5 * x * (1.0 + jnp.tanh(0.7978845608028654 * (x + 0.044715 * x * x * x)))


def _mm(a, b):
    return jnp.dot(a.astype(BF16), b.astype(BF16), preferred_element_type=F32)


def _mm_nt(a, b):
    return lax.dot_general(a.astype(BF16), b.astype(BF16), (((1,), (1,)), ((), ())),
                           preferred_element_type=F32)


def _mm_tn(a, b):
    return lax.dot_general(a.astype(BF16), b.astype(BF16), (((0,), (0,)), ((), ())),
                           preferred_element_type=F32)


def _split(x):
    hi = x.astype(BF16)
    lo = (x - hi.astype(F32)).astype(BF16)
    return hi, lo


def _mm3(a, b):
    ah, al = _split(a)
    bh, bl = _split(b)
    d = functools.partial(jnp.dot, preferred_element_type=F32)
    return d(ah, bh) + d(ah, bl) + d(al, bh)


def _mm_exact_rhs(a, b_bf16):
    d = functools.partial(jnp.dot, preferred_element_type=F32)
    a0 = a.astype(BF16)
    r1 = a - a0.astype(F32)
    a1 = r1.astype(BF16)
    a2 = (r1 - a1.astype(F32)).astype(BF16)
    return d(a0, b_bf16) + d(a1, b_bf16) + d(a2, b_bf16)


def _mm_exact_lhs(a_bf16, b):
    d = functools.partial(jnp.dot, preferred_element_type=F32)
    b0 = b.astype(BF16)
    r1 = b - b0.astype(F32)
    b1 = r1.astype(BF16)
    b2 = (r1 - b1.astype(F32)).astype(BF16)
    return d(a_bf16, b0) + d(a_bf16, b1) + d(a_bf16, b2)


def _group_matrix(value):
    i = lax.broadcasted_iota(jnp.int32, (BR_W, BR_W), 0) // HEAD_DIM
    j = lax.broadcasted_iota(jnp.int32, (BR_W, BR_W), 1) // HEAD_DIM
    return jnp.where(i == j, value, 0.0).astype(BF16)


def _head_of_lane(shape):
    return lax.broadcasted_iota(jnp.int32, shape, len(shape) - 1) // HEAD_DIM


def _norm_mod(x, g, shift, scale):
    ms = jnp.mean(x * x, axis=-1, keepdims=True)
    return (x * lax.rsqrt(ms + EPS) * g) * (1.0 + scale) + shift


def _mod_kernel(cs_ref, w_ref, b_ref, o_ref):
    o_ref[0] = _mm3(_silu(cs_ref[...]), w_ref[0]) + b_ref[0]


def _modulation(cs, ada_w, ada_b):
    depth, d, d3 = ada_w.shape
    rows = cs.shape[0]
    tn = 768
    return pl.pallas_call(
        _mod_kernel,
        grid=(depth, d3 // tn),
        in_specs=[pl.BlockSpec((rows, d), lambda l, j: (0, 0)),
                  pl.BlockSpec((1, d, tn), lambda l, j: (l, 0, j)),
                  pl.BlockSpec((1, 1, tn), lambda l, j: (l, 0, j))],
        out_specs=pl.BlockSpec((1, rows, tn), lambda l, j: (l, 0, j)),
        out_shape=jax.ShapeDtypeStruct((depth, rows, d3), F32),
        compiler_params=_params("arbitrary", "arbitrary"),
        name="adaln_modulation",
    )(cs, ada_w, ada_b.reshape(depth, 1, d3))


def _proj_kernel(x_ref, g_ref, sh_ref, sc_ref, w_ref, gm_ref, na_ref, rwin_ref, rwg_ref, pl_ref):
    h = _norm_mod(x_ref[0], g_ref[...], sh_ref[0], sc_ref[0]).astype(BF16)
    off = 0
    for ref in (gm_ref, na_ref, rwin_ref, rwg_ref, pl_ref):
        width = ref.shape[-1]
        ref[0] = jnp.dot(h, w_ref[:, off:off + width], preferred_element_type=F32).astype(ref.dtype)
        off += width


def _project(x, norm_g, shift, scale, w_mix):
    b, t, d = x.shape
    tm = min(t, 256)
    widths = (SEG_GM, SEG_NA, SEG_RWIN, SEG_RWG, SEG_PL)
    dtypes = (BF16, BF16, F32, BF16, BF16)
    row = lambda i, j: (i, j, 0)
    per_b = lambda i, j: (i, 0, 0)
    return pl.pallas_call(
        _proj_kernel,
        grid=(b, t // tm),
        in_specs=[pl.BlockSpec((1, tm, d), row),
                  pl.BlockSpec((1, d), lambda i, j: (0, 0)),
                  pl.BlockSpec((1, 1, d), per_b),
                  pl.BlockSpec((1, 1, d), per_b),
                  pl.BlockSpec((d, MIX_COLS), lambda i, j: (0, 0))],
        out_specs=[pl.BlockSpec((1, tm, w), row) for w in widths],
        out_shape=[jax.ShapeDtypeStruct((b, t, w), dt) for w, dt in zip(widths, dtypes)],
        compiler_params=_params("parallel", "parallel"),
        name="norm_in_proj",
    )(x, norm_g.reshape(1, d), shift, scale, w_mix)


def _gmlp_kernel(z_ref, lng_ref, lnb_ref, ws_ref, bs_ref, o_ref):
    z = z_ref[0].astype(F32)
    tm = z.shape[0]
    u = _gelu(z[:, 0:BR_W])
    v = _gelu(z[:, BR_W:2 * BR_W])
    gate = z[:, 2 * BR_W:3 * BR_W]
    avg = _group_matrix(1.0 / HEAD_DIM)
    mu = _mm_exact_rhs(v, avg)
    vc = v - mu
    var = _mm_exact_rhs(vc * vc, avg)
    vh = (vc * lax.rsqrt(var + EPS) * lng_ref[...] + lnb_ref[...]).astype(BF16)
    head = _head_of_lane((CHUNK, BR_W))
    su = u * _silu(gate)
    for c in range(tm // CHUNK):
        rows = slice(c * CHUNK, (c + 1) * CHUNK)
        vch = vh[rows]
        mixed = jnp.zeros((CHUNK, BR_W), F32)
        for g in range(N_HEADS):
            mg = jnp.dot(ws_ref[g], vch, preferred_element_type=F32)
            mixed = jnp.where(head == g, mg, mixed)
        o_ref[0, rows, :] = (su[rows] * (mixed + bs_ref[...])).astype(o_ref.dtype)


def _gmlp(gm, ln_g, ln_b, w_s, b_s):
    b, t, _ = gm.shape
    tm = min(t, 512)
    bs_full = jnp.repeat(b_s.T, HEAD_DIM, axis=1)
    const2 = lambda i, j: (0, 0)
    return pl.pallas_call(
        _gmlp_kernel,
        grid=(b, t // tm),
        in_specs=[pl.BlockSpec((1, tm, SEG_GM), lambda i, j: (i, j, 0)),
                  pl.BlockSpec((1, BR_W), const2),
                  pl.BlockSpec((1, BR_W), const2),
                  pl.BlockSpec((N_HEADS, CHUNK, CHUNK), lambda i, j: (0, 0, 0)),
                  pl.BlockSpec((CHUNK, BR_W), const2)],
        out_specs=pl.BlockSpec((1, tm, BR_W), lambda i, j: (i, j, 0)),
        out_shape=jax.ShapeDtypeStruct((b, t, BR_W), BF16),
        compiler_params=_params("parallel", "parallel"),
        name="gmlp_mixer",
    )(gm, ln_g.reshape(1, BR_W), ln_b.reshape(1, BR_W), w_s.astype(BF16), bs_full)


def _pool_kernel(p_ref, g_ref, w_ref, sc_ref, o_ref, ext_ref, *, seq_len):
    j = pl.program_id(1)
    tm = o_ref.shape[1]
    n = tm + 2 * POOL_HALO

    @pl.when(j == 0)
    def _():
        ext_ref[0:POOL_HALO, :] = jnp.zeros((POOL_HALO, BR_W), F32)
        ext_ref[POOL_HALO + seq_len:, :] = jnp.zeros((POOL_HALO, BR_W), F32)
        ext_ref[POOL_HALO:POOL_HALO + seq_len, :] = p_ref[0].astype(F32)

    t0 = pl.multiple_of(j * tm, 8)
    x = ext_ref[pl.ds(t0, n), :]
    s2 = x + pltpu.roll(x, 1, 0)
    s4 = pltpu.roll(s2, 1, 0) + pltpu.roll(s2, n - 1, 0)
    s8 = pltpu.roll(s4, 2, 0) + pltpu.roll(s4, n - 2, 0)
    s16 = pltpu.roll(s8, 4, 0) + pltpu.roll(s8, n - 4, 0)
    mid = slice(POOL_HALO, POOL_HALO + tm)
    grp = _head_of_lane((tm, BR_W))
    ssum = jnp.where(grp == 0, s2[mid], jnp.where(grp == 1, s4[mid], jnp.where(grp == 2, s8[mid], s16[mid])))
    half = jnp.where(grp == 0, 1, jnp.where(grp == 1, 2, jnp.where(grp == 2, 4, 8)))
    t = t0 + lax.broadcasted_iota(jnp.int32, (tm, BR_W), 0)
    cnt = jnp.minimum(t + half, seq_len) - jnp.maximum(t - half, 0)
    d = ssum / cnt.astype(F32) - x[mid]
    y = _mm(d, w_ref[...])
    o_ref[0] = (y * sc_ref[...] * _silu(g_ref[0].astype(F32))).astype(o_ref.dtype)


def _pool(plz, pl_w, pl_scale):
    b, t, _ = plz.shape
    tm = min(t, 512)
    w_bd = jax.scipy.linalg.block_diag(*[pl_w[g] for g in range(N_HEADS)]).astype(BF16)
    return pl.pallas_call(
        functools.partial(_pool_kernel, seq_len=t),
        grid=(b, t // tm),
        in_specs=[pl.BlockSpec((1, t, BR_W), lambda i, j: (i, 0, 0)),
                  pl.BlockSpec((1, tm, BR_W), lambda i, j: (i, j, 1)),
                  pl.BlockSpec((BR_W, BR_W), lambda i, j: (0, 0)),
                  pl.BlockSpec((1, BR_W), lambda i, j: (0, 0))],
        out_specs=pl.BlockSpec((1, tm, BR_W), lambda i, j: (i, j, 0)),
        out_shape=jax.ShapeDtypeStruct((b, t, BR_W), BF16),
        scratch_shapes=[pltpu.VMEM((t + 2 * POOL_HALO, BR_W), F32)],
        compiler_params=_params("parallel", "arbitrary"),
        name="pool_mixer",
    )(plz, plz, w_bd, pl_scale.reshape(1, BR_W))


def _stack_heads(q):
    head = _head_of_lane(q.shape)
    return jnp.concatenate([jnp.where(head == h, q, 0.0) for h in range(N_HEADS)], axis=0)


def _unstack_heads(o, n):
    head = _head_of_lane((n, BR_W))
    out = jnp.zeros((n, BR_W), F32)
    for h in range(N_HEADS):
        out = jnp.where(head == h, o[h * n:(h + 1) * n], out)
    return out


def _natten_kernel(q_ref, k_ref, v_ref, g_ref, kc_ref, vc_ref, bias_ref, o_ref, *, rows):
    band = NA_ROWS * GRID_W
    rng = range(NA_RT)
    kc = kc_ref[0]
    vc = vc_ref[0]
    r = [pl.program_id(1) * NA_RT + i for i in rng]
    rs = [jnp.clip(r[i] - NA_ROWS // 2, 0, rows - NA_ROWS) for i in rng]
    start = [pl.multiple_of(rs[i] * GRID_W, GRID_W) for i in rng]
    qs = [_stack_heads(q_ref[0, i * GRID_W:(i + 1) * GRID_W, :] * (HEAD_DIM ** -0.5)) for i in rng]
    kb = [k_ref[0, pl.ds(start[i], band), :] for i in rng]
    vb = [v_ref[0, pl.ds(start[i], band), :] for i in rng]
    s_b = [_mm_nt(qs[i], kb[i]) + bias_ref[rs[i] - r[i] + NA_ROWS - 1] for i in rng]
    s_c = [_mm_nt(qs[i], kc) for i in rng]
    m = [jnp.maximum(jnp.max(s_b[i], axis=1, keepdims=True), jnp.max(s_c[i], axis=1, keepdims=True)) for i in rng]
    e_b = [jnp.exp(s_b[i] - m[i]) for i in rng]
    e_c = [jnp.exp(s_c[i] - m[i]) for i in rng]
    den = [jnp.sum(e_b[i], axis=1, keepdims=True) + jnp.sum(e_c[i], axis=1, keepdims=True) for i in rng]
    pv = [(_mm(e_b[i], vb[i]) + _mm(e_c[i], vc)) / den[i] for i in rng]
    for i in rng:
        rows_i = slice(i * GRID_W, (i + 1) * GRID_W)
        gate = _silu(g_ref[0, rows_i, :].astype(F32))
        o_ref[0, rows_i, :] = (_unstack_heads(pv[i], GRID_W) * gate).astype(o_ref.dtype)


def _natten_bias(rpb):
    c = jnp.arange(GRID_W)
    c_start = jnp.clip(c - NA_COLS // 2, 0, GRID_W - NA_COLS)
    col_ok = (c[None, :] >= c_start[:, None]) & (c[None, :] < c_start[:, None] + NA_COLS)
    dc = jnp.clip(c[None, :] - c[:, None], -(NA_COLS - 1), NA_COLS - 1) + NA_COLS - 1
    dr = jnp.arange(NA_ROWS)[:, None] + jnp.arange(NA_ROWS)[None, :]
    tab = rpb[:, dr][..., dc]
    tab = jnp.where(col_ok[None, None, None], tab, NEG)
    tab = tab.transpose(1, 0, 3, 2, 4)
    return tab.reshape(NA_ROWS, N_HEADS * GRID_W, NA_ROWS * GRID_W)


def _natten(na, na_ctx, rpb):
    b, t, _ = na.shape
    lc = na_ctx.shape[1]
    rows = t // GRID_W
    band = NA_ROWS * GRID_W
    tq = NA_RT * GRID_W
    qrow = lambda c: (lambda i, r: (i, r, c))
    full = lambda c: (lambda i, r: (i, 0, c))
    return pl.pallas_call(
        functools.partial(_natten_kernel, rows=rows),
        grid=(b, rows // NA_RT),
        in_specs=[pl.BlockSpec((1, tq, BR_W), qrow(0)),
                  pl.BlockSpec((1, t, BR_W), full(1)),
                  pl.BlockSpec((1, t, BR_W), full(2)),
                  pl.BlockSpec((1, tq, BR_W), qrow(3)),
                  pl.BlockSpec((1, lc, BR_W), full(1)),
                  pl.BlockSpec((1, lc, BR_W), full(2)),
                  pl.BlockSpec((NA_ROWS, N_HEADS * GRID_W, band), lambda i, r: (0, 0, 0))],
        out_specs=pl.BlockSpec((1, tq, BR_W), qrow(0)),
        out_shape=jax.ShapeDtypeStruct((b, t, BR_W), BF16),
        compiler_params=_params("parallel", "arbitrary"),
        name="natten_mixer",
    )(na, na, na, na, na_ctx, na_ctx, _natten_bias(rpb))


def _ctx_attn_kernel(q_ref, k_ref, v_ref, g_ref, o_ref):
    n = q_ref.shape[1]
    qs = _stack_heads(q_ref[0] * (HEAD_DIM ** -0.5))
    s = _mm_nt(qs, k_ref[0])
    e = jnp.exp(s - jnp.max(s, axis=1, keepdims=True))
    pv = _mm(e, v_ref[0]) / jnp.sum(e, axis=1, keepdims=True)
    o_ref[0] = (_unstack_heads(pv, n) * _silu(g_ref[0].astype(F32))).astype(o_ref.dtype)


def _ctx_attention(na_ctx):
    b, lc, _ = na_ctx.shape
    tq = 64
    qrow = lambda c: (lambda i, j: (i, j, c))
    full = lambda c: (lambda i, j: (i, 0, c))
    return pl.pallas_call(
        _ctx_attn_kernel,
        grid=(b, lc // tq),
        in_specs=[pl.BlockSpec((1, tq, BR_W), qrow(0)),
                  pl.BlockSpec((1, lc, BR_W), full(1)),
                  pl.BlockSpec((1, lc, BR_W), full(2)),
                  pl.BlockSpec((1, tq, BR_W), qrow(3))],
        out_specs=pl.BlockSpec((1, tq, BR_W), qrow(0)),
        out_shape=jax.ShapeDtypeStruct((b, lc, BR_W), BF16),
        compiler_params=_params("parallel", "arbitrary"),
        name="ctx_attention",
    )(na_ctx, na_ctx, na_ctx, na_ctx)


PREP_R, PREP_V, PREP_A = 0, 1, 2
PREP_DIR = 3
PREP_BONUS = 9
PREP_COLS = 10 * BR_W


def _softplus(x):
    return jnp.maximum(x, 0.0) + jnp.log(1.0 + jnp.exp(-jnp.abs(x)))


def _prep_kernel(z_ref, zp_ref, zn_ref, mu_ref, kkw_ref, ka_ref, rk_ref, w0_ref, a0_ref, w2_ref, a2_ref, o_ref):
    j = pl.program_id(1)
    z = z_ref[0]
    tp = z.shape[0]
    row = lax.broadcasted_iota(jnp.int32, z.shape, 0)
    prev_row = jnp.where(j > 0, zp_ref[0, 7:8, :], 0.0)
    next_row = jnp.where(j < pl.num_programs(1) - 1, zn_ref[0, 0:1, :], 0.0)
    prev = jnp.where(row == 0, prev_row, pltpu.roll(z, 1, 0))
    nxt = jnp.where(row == tp - 1, next_row, pltpu.roll(z, tp - 1, 0))
    zf = z + (0.5 * (prev + nxt) - z) * mu_ref[...]
    r = zf[:, 0:BR_W]
    k = zf[:, BR_W:2 * BR_W]
    v = zf[:, 2 * BR_W:3 * BR_W]
    lo = zf[:, 3 * BR_W:3 * BR_W + 2 * LORA]
    ones = _group_matrix(1.0)
    kk = k * kkw_ref[...]
    kk = kk / jnp.maximum(jnp.sqrt(_mm_exact_rhs(kk * kk, ones)), 1e-12)
    o_ref[0, :, PREP_R * BR_W:(PREP_R + 1) * BR_W] = r
    o_ref[0, :, PREP_V * BR_W:(PREP_V + 1) * BR_W] = v
    o_ref[0, :, PREP_A * BR_W:(PREP_A + 1) * BR_W] = -kk
    tlo = jnp.tanh(lo)
    kd_sum = jnp.zeros_like(k)
    for d in range(2):
        w = -_softplus(-(w0_ref[d:d + 1, :] + _mm3(tlo, w2_ref[d]))) - 0.5
        a = _sigmoid(a0_ref[d:d + 1, :] + _mm3(lo, a2_ref[d]))
        kd = k * (1.0 + (a - 1.0) * ka_ref[...])
        c0 = (PREP_DIR + 3 * d) * BR_W
        o_ref[0, :, c0:c0 + BR_W] = -jnp.exp(w)
        o_ref[0, :, c0 + BR_W:c0 + 2 * BR_W] = kd
        o_ref[0, :, c0 + 2 * BR_W:c0 + 3 * BR_W] = kk * a
        kd_sum = kd_sum + kd
    bonus = _mm_exact_rhs(r * kd_sum * rk_ref[...], ones) * v
    o_ref[0, :, PREP_BONUS * BR_W:] = bonus


def _rwkv_prep(rwin, lp):
    b, t, width = rwin.shape
    tp = min(t, 256)
    nblk8 = t // 8
    zeros = jnp.zeros((2, LORA, BR_W), F32)
    w2 = jnp.concatenate([lp['rw_w2'], zeros], axis=1)
    a2 = jnp.concatenate([zeros, lp['rw_a2']], axis=1)
    vec = lambda i, j: (0, 0)
    return pl.pallas_call(
        _prep_kernel,
        grid=(b, t // tp),
        in_specs=[pl.BlockSpec((1, tp, width), lambda i, j: (i, j, 0)),
                  pl.BlockSpec((1, 8, width), lambda i, j: (i, jnp.maximum(j * (tp // 8) - 1, 0), 0)),
                  pl.BlockSpec((1, 8, width), lambda i, j: (i, jnp.minimum((j + 1) * (tp // 8), nblk8 - 1), 0)),
                  pl.BlockSpec((1, width), vec),
                  pl.BlockSpec((1, BR_W), vec),
                  pl.BlockSpec((1, BR_W), vec),
                  pl.BlockSpec((1, BR_W), vec),
                  pl.BlockSpec((2, BR_W), vec),
                  pl.BlockSpec((2, BR_W), vec),
                  pl.BlockSpec((2, 2 * LORA, BR_W), lambda i, j: (0, 0, 0)),
                  pl.BlockSpec((2, 2 * LORA, BR_W), lambda i, j: (0, 0, 0))],
        out_specs=pl.BlockSpec((1, tp, PREP_COLS), lambda i, j: (i, j, 0)),
        out_shape=jax.ShapeDtypeStruct((b, t, PREP_COLS), F32),
        compiler_params=_params("parallel", "parallel"),
        name="rwkv_prep",
    )(rwin, rwin, rwin, lp['rw_mu'].reshape(1, width), lp['rw_kk'].reshape(1, BR_W),
      lp['rw_ka'].reshape(1, BR_W), lp['rw_rk'].reshape(1, BR_W), lp['rw_w0'], lp['rw_a0'], w2, a2)


def _stack_pair(x):
    head = _head_of_lane(x.shape)
    return jnp.concatenate([jnp.where(head == 0, x, 0.0), jnp.where(head == 1, x, 0.0)], axis=0)


def _scan_chunks(chains):
    n = chains[0][0].shape[0]
    m = 2 * n
    nc = len(chains)
    rng = range(nc)
    rev = [c[7] for c in chains]
    ti = lax.broadcasted_iota(jnp.int32, (n, n), 0)
    si = lax.broadcasted_iota(jnp.int32, (n, n), 1)
    tri = {False: jnp.where(si <= ti, 1.0, 0.0).astype(BF16), True: jnp.where(si >= ti, 1.0, 0.0).astype(BF16)}
    tri3 = {k: jnp.concatenate([t, t, t], axis=1) for k, t in tri.items()}
    i2 = lax.broadcasted_iota(jnp.int32, (m, 2 * m), 0)
    j2 = lax.broadcasted_iota(jnp.int32, (m, 2 * m), 1) % n
    t2 = i2 % n
    same = jnp.where(i2 < n, 0, 1)
    keep = {False: j2 < t2 + same, True: j2 > t2 - same}
    ri = lax.broadcasted_iota(jnp.int32, (LANES, LANES), 0)
    ci = lax.broadcasted_iota(jnp.int32, (LANES, LANES), 1)
    eye = ri == ci
    same_head = (ri // HEAD_DIM) == (ci // HEAD_DIM)
    d = functools.partial(jnp.dot, preferred_element_type=F32)

    def pieces3(x):
        x0 = x.astype(BF16)
        r1 = x - x0.astype(F32)
        x1 = r1.astype(BF16)
        return jnp.concatenate([x0, x1, (r1 - x1.astype(F32)).astype(BF16)], axis=0)

    cum = [d(tri3[rev[c]], pieces3(chains[c][3])) for c in rng]
    tot = [jnp.sum(chains[c][3], axis=0, keepdims=True) for c in rng]
    e_neg = [jnp.exp(-cum[c]) for c in rng]
    ar = [jnp.concatenate([chains[c][2] * jnp.exp(cum[c] - chains[c][3]),
                           chains[c][0] * jnp.exp(cum[c])], axis=0).astype(BF16) for c in rng]
    bk_st = [jnp.concatenate([_stack_pair((chains[c][5] * e_neg[c]).astype(BF16)),
                              _stack_pair((chains[c][4] * e_neg[c]).astype(BF16))], axis=0) for c in rng]
    v_bf = [chains[c][1].astype(BF16) for c in rng]

    g = [jnp.where(keep[rev[c]], _mm_nt(ar[c], bk_st[c]), 0.0) for c in rng]
    lhs = [jnp.concatenate([ar[c], g[c][:, m:].astype(BF16)], axis=1) for c in rng]
    rhs = [jnp.concatenate([chains[c][6].astype(BF16), _stack_pair(v_bf[c])], axis=0) for c in rng]
    xy = [d(lhs[c], rhs[c]) for c in rng]

    u = [xy[c][:n] for c in rng]
    p = [g[c][:n, :m] for c in rng]
    step = 1
    while step < n:
        step *= 2
        last = step >= n
        nxt_u, nxt_p = [], []
        for c in rng:
            p_bf = p[c].astype(BF16)
            u_hi, u_lo = _split(u[c])
            rhs_parts = [_stack_pair(u_hi), _stack_pair(u_lo)] + ([] if last else [_stack_pair(p_bf)])
            acc = d(p_bf, jnp.concatenate(rhs_parts, axis=1))
            nxt_u.append(u[c] + (acc[:, :LANES] + acc[:, LANES:2 * LANES]))
            nxt_p.append(None if last else acc[:, 2 * LANES:])
        u, p = nxt_u, nxt_p

    outs = []
    for c in rng:
        u_bf = u[c].astype(BF16)
        y = xy[c][n:] + d(g[c][n:, :m].astype(BF16), _stack_pair(u_bf))
        e_rest = jnp.exp(tot[c] - cum[c])
        bk_rest = jnp.concatenate([chains[c][5] * e_rest, chains[c][4] * e_rest], axis=0)
        decay = jnp.sum(jnp.where(eye, jnp.exp(tot[c]), 0.0), axis=1, keepdims=True)
        upd = _mm_tn(bk_rest, jnp.concatenate([u_bf, v_bf[c]], axis=0))
        outs.append((y, chains[c][6] * decay + jnp.where(same_head, upd, 0.0)))
    return outs


def _scan_kernel(sf_ref, df_ref, sb_ref, db_ref, s0_ref, yf_ref, yb_ref, sfin_ref, state_ref):
    j = pl.program_id(1)

    @pl.when(j == 0)
    def _():
        state_ref[...] = s0_ref[...]

    nb = sf_ref.shape[0]
    npair = BR_W // LANES
    chains = []
    for i in range(nb):
        for dr, (s_ref, d_ref) in enumerate(((sf_ref, df_ref), (sb_ref, db_ref))):
            for p in range(npair):
                col = lambda blk: slice(blk * BR_W + p * LANES, blk * BR_W + (p + 1) * LANES)
                chains.append((s_ref[i, :, col(PREP_R)], s_ref[i, :, col(PREP_V)], s_ref[i, :, col(PREP_A)],
                               d_ref[i, :, col(0)], d_ref[i, :, col(1)], d_ref[i, :, col(2)],
                               state_ref[i, dr, p], dr == 1))
    outs = iter(_scan_chunks(chains))
    for i in range(nb):
        for dr, y_ref in enumerate((yf_ref, yb_ref)):
            for p in range(npair):
                y, new_state = next(outs)
                y_ref[i, :, p * LANES:(p + 1) * LANES] = y
                state_ref[i, dr, p] = new_state

    @pl.when(j == pl.num_programs(1) - 1)
    def _():
        sfin_ref[...] = state_ref[...]


def _rwkv_scan(prep, s0):
    b, t, _ = prep.shape
    n = t // SCAN_L
    npair = BR_W // LANES
    blk3 = 3 * BR_W
    fwd = lambda c: (lambda i, j: (i, j, c))
    bwd = lambda c: (lambda i, j: (i, n - 1 - j, c))
    st = lambda i, j: (i, 0, 0, 0, 0)
    nb = SCAN_NB if b % SCAN_NB == 0 else 1
    return pl.pallas_call(
        _scan_kernel,
        grid=(b // nb, n),
        in_specs=[pl.BlockSpec((nb, SCAN_L, blk3), fwd(0)),
                  pl.BlockSpec((nb, SCAN_L, blk3), fwd(1)),
                  pl.BlockSpec((nb, SCAN_L, blk3), bwd(0)),
                  pl.BlockSpec((nb, SCAN_L, blk3), bwd(2)),
                  pl.BlockSpec((nb, 2, npair, LANES, LANES), st)],
        out_specs=[pl.BlockSpec((nb, SCAN_L, BR_W), fwd(0)),
                   pl.BlockSpec((nb, SCAN_L, BR_W), bwd(0)),
                   pl.BlockSpec((nb, 2, npair, LANES, LANES), st)],
        out_shape=[jax.ShapeDtypeStruct((b, t, BR_W), F32),
                   jax.ShapeDtypeStruct((b, t, BR_W), F32),
                   jax.ShapeDtypeStruct((b, 2, npair, LANES, LANES), F32)],
        scratch_shapes=[pltpu.VMEM((nb, 2, npair, LANES, LANES), F32)],
        compiler_params=_params("parallel", "arbitrary"),
        name="rwkv_scan",
    )(prep, prep, prep, prep, s0)


def _rwkv_out_kernel(yf_ref, yb_ref, bonus_ref, g_ref, gng_ref, gnb_ref, o_ref):
    o = yf_ref[0] + yb_ref[0]
    avg = _group_matrix(1.0 / HEAD_DIM)
    mu = _mm_exact_rhs(o, avg)
    oc = o - mu
    var = _mm_exact_rhs(oc * oc, avg)
    o = oc * lax.rsqrt(var + GN_EPS) * gng_ref[...] + gnb_ref[...] + bonus_ref[0]
    o_ref[0] = (o * _silu(g_ref[0].astype(F32))).astype(o_ref.dtype)


def _rwkv_out(yf, yb, prep, rwg, gn_g, gn_b):
    b, t, _ = yf.shape
    tm = min(t, 512)
    row = lambda i, j: (i, j, 0)
    vec = lambda i, j: (0, 0)
    return pl.pallas_call(
        _rwkv_out_kernel,
        grid=(b, t // tm),
        in_specs=[pl.BlockSpec((1, tm, BR_W), row),
                  pl.BlockSpec((1, tm, BR_W), row),
                  pl.BlockSpec((1, tm, BR_W), lambda i, j: (i, j, PREP_BONUS)),
                  pl.BlockSpec((1, tm, BR_W), row),
                  pl.BlockSpec((1, BR_W), vec),
                  pl.BlockSpec((1, BR_W), vec)],
        out_specs=pl.BlockSpec((1, tm, BR_W), row),
        out_shape=jax.ShapeDtypeStruct((b, t, BR_W), BF16),
        compiler_params=_params("parallel", "parallel"),
        name="rwkv_groupnorm_gate",
    )(yf, yb, prep, rwg, gn_g.reshape(1, BR_W), gn_b.reshape(1, BR_W))


def _merge_kernel(x_ref, g_ref, sh_ref, sc_ref, gt_ref, gm_ref, na_ref, rw_ref, pl_ref,
                  wm_ref, wbr_ref, wout_ref, fin_ref, o_ref, *, final):
    x = x_ref[0]
    d = x.shape[-1]
    h = _norm_mod(x, g_ref[...], sh_ref[0], sc_ref[0]).astype(BF16)
    y = None
    for i, br_ref in enumerate((gm_ref, na_ref, rw_ref, pl_ref)):
        logits = jnp.dot(h, wm_ref[:, i * d:(i + 1) * d], preferred_element_type=F32)
        term = _sigmoid(logits) * jnp.dot(br_ref[0].astype(BF16), wbr_ref[i], preferred_element_type=F32)
        y = term if y is None else y + term
    out = x + gt_ref[0] * jnp.dot(y.astype(BF16), wout_ref[...], preferred_element_type=F32)
    if final:
        ms = jnp.mean(out * out, axis=-1, keepdims=True)
        out = out * lax.rsqrt(ms + EPS) * fin_ref[...]
    o_ref[0] = out


def _merge(x, norm_g, shift, scale, gate, branches, w_merge, w_br, w_out, final_g, final):
    b, t, d = x.shape
    tm = min(t, 256)
    row = lambda i, j: (i, j, 0)
    per_b = lambda i, j: (i, 0, 0)
    vec = lambda i, j: (0, 0)
    return pl.pallas_call(
        functools.partial(_merge_kernel, final=final),
        grid=(b, t // tm),
        in_specs=[pl.BlockSpec((1, tm, d), row),
                  pl.BlockSpec((1, d), vec),
                  pl.BlockSpec((1, 1, d), per_b),
                  pl.BlockSpec((1, 1, d), per_b),
                  pl.BlockSpec((1, 1, d), per_b)]
                 + [pl.BlockSpec((1, tm, BR_W), row)] * 4
                 + [pl.BlockSpec(w_merge.shape, vec),
                    pl.BlockSpec(w_br.shape, lambda i, j: (0, 0, 0)),
                    pl.BlockSpec(w_out.shape, vec),
                    pl.BlockSpec((1, d), vec)],
        out_specs=pl.BlockSpec((1, tm, d), row),
        out_shape=jax.ShapeDtypeStruct((b, t, d), F32),
        compiler_params=_params("parallel", "parallel"),
        name="merge_out_proj",
    )(x, norm_g.reshape(1, d), shift, scale, gate, *branches, w_merge, w_br, w_out, final_g.reshape(1, d))


def _rwkv_branch(rwin, rwg, s0, lp):
    prep = _rwkv_prep(rwin, lp)
    yf, yb, s_fin = _rwkv_scan(prep, s0)
    return functools.partial(_rwkv_out, yf, yb, prep, rwg, lp['rw_gn_g'], lp['rw_gn_b']), s_fin


def _layer(x, ctx, mod_x, mod_c, lp, final_g, last):
    b, _, d = x.shape
    w_in = lp['w_in']
    w_mix = w_in[:, :MIX_COLS].astype(BF16)
    w_merge = w_in[:, MIX_COLS:].astype(BF16)
    w_br = lp['w_br'].astype(BF16)
    w_out = lp['w_out'].astype(BF16)

    gm_c, na_c, rwin_c, rwg_c, pl_c = _project(ctx, lp['norm_g'], mod_c[0], mod_c[1], w_mix)
    gm_x, na_x, rwin_x, rwg_x, pl_x = _project(x, lp['norm_g'], mod_x[0], mod_x[1], w_mix)

    zero_state = jnp.zeros((b, 2, BR_W // LANES, LANES, LANES), F32)
    rw_c_out, s_ctx = _rwkv_branch(rwin_c, rwg_c, zero_state, lp)
    rw_x_out, _ = _rwkv_branch(rwin_x, rwg_x, s_ctx, lp)

    branches = [_gmlp(gm_x, lp['gm_ln_g'], lp['gm_ln_b'], lp['gm_ws'], lp['gm_bs']),
                _natten(na_x, na_c, lp['na_rpb']),
                rw_x_out(),
                _pool(pl_x, lp['pl_w'], lp['pl_scale'])]
    x_new = _merge(x, lp['norm_g'], mod_x[0], mod_x[1], mod_x[2], branches,
                   w_merge, w_br, w_out, final_g, final=last)
    if last:
        return x_new, ctx
    branches_c = [_gmlp(gm_c, lp['gm_ln_g'], lp['gm_ln_b'], lp['gm_ws'], lp['gm_bs']),
                  _ctx_attention(na_c),
                  rw_c_out(),
                  _pool(pl_c, lp['pl_w'], lp['pl_scale'])]
    ctx_new = _merge(ctx, lp['norm_g'], mod_c[0], mod_c[1], mod_c[2], branches_c,
                     w_merge, w_br, w_out, final_g, final=False)
    return x_new, ctx_new


def kernel(x, c, ctx, c_ctx, ada_w, ada_b, norm_g, w_in, gm_ln_g, gm_ln_b, gm_ws, gm_bs, na_rpb, rw_mu, rw_w0, rw_w2, rw_a0, rw_a2, rw_kk, rw_ka, rw_rk, rw_gn_g, rw_gn_b, pl_w, pl_scale, w_br, w_out, final_g):
    b, _, d = x.shape
    depth = ada_w.shape[0]
    mod_rows = 16
    cs = jnp.zeros((mod_rows, d), F32).at[:b].set(c).at[b].set(c_ctx)
    mod = _modulation(cs, ada_w, ada_b)
    for l in range(depth):
        lp = {
            'norm_g': norm_g[l], 'w_in': w_in[l],
            'gm_ln_g': gm_ln_g[l], 'gm_ln_b': gm_ln_b[l], 'gm_ws': gm_ws[l], 'gm_bs': gm_bs[l],
            'na_rpb': na_rpb[l],
            'rw_mu': rw_mu[l], 'rw_w0': rw_w0[l], 'rw_w2': rw_w2[l], 'rw_a0': rw_a0[l], 'rw_a2': rw_a2[l],
            'rw_kk': rw_kk[l], 'rw_ka': rw_ka[l], 'rw_rk': rw_rk[l], 'rw_gn_g': rw_gn_g[l], 'rw_gn_b': rw_gn_b[l],
            'pl_w': pl_w[l], 'pl_scale': pl_scale[l], 'w_br': w_br[l], 'w_out': w_out[l],
        }
        mod_x = [mod[l, :b, i * d:(i + 1) * d].reshape(b, 1, d) for i in range(3)]
        mod_c = [jnp.broadcast_to(mod[l, b, i * d:(i + 1) * d].reshape(1, 1, d), (b, 1, d)) for i in range(3)]
        x, ctx = _layer(x, ctx, mod_x, mod_c, lp, final_g, last=(l == depth - 1))
    return x
```

```python
import functools

import jax
import jax.numpy as jnp
import numpy as np
from jax import lax
from jax.experimental import pallas as pl
from jax.experimental.pallas import tpu as pltpu

F32 = jnp.float32
BF16 = jnp.bfloat16

N_HEADS = 4
N_BRANCH = 4
HEAD_DIM = 64
BR_W = N_HEADS * HEAD_DIM
LANES = 128
GRID_W = 64
NA_ROWS = 8
NA_COLS = 16
NA_RT = 4
CHUNK = 128
LORA = 64
POOL_WINDOWS = (2, 4, 8, 16)
POOL_HALO = 16
SCAN_L = 64
SCAN_NB = 4
EPS = 1e-6
GN_EPS = 64e-5
NEG = -1e30
VMEM_LIMIT = 56 * 1024 * 1024

SEG_GM, SEG_NA, SEG_RWIN, SEG_RWG, SEG_PL = 3 * BR_W, 4 * BR_W, 3 * BR_W + 2 * LORA, BR_W, 2 * BR_W
MIX_COLS = SEG_GM + SEG_NA + SEG_RWIN + SEG_RWG + SEG_PL


def _params(*sem):
    return pltpu.CompilerParams(dimension_semantics=sem, vmem_limit_bytes=VMEM_LIMIT)


def _sigmoid(x):
    return 1.0 / (1.0 + jnp.exp(-x))


def _silu(x):
    return x * _sigmoid(x)


def _gelu(x):
    return 0.5 * x * (1.0 + jnp.tanh(0.7978845608028654 * (x + 0.044715 * x * x * x)))


def _mm(a, b):
    return jnp.dot(a.astype(BF16), b.astype(BF16), preferred_element_type=F32)


def _mm_nt(a, b):
    return lax.dot_general(a.astype(BF16), b.astype(BF16), (((1,), (1,)), ((), ())),
                           preferred_element_type=F32)


def _mm_tn(a, b):
    return lax.dot_general(a.astype(BF16), b.astype(BF16), (((0,), (0,)), ((), ())),
                           preferred_element_type=F32)


def _split(x):
    hi = x.astype(BF16)
    lo = (x - hi.astype(F32)).astype(BF16)
    return hi, lo


def _mm3(a, b):
    ah, al = _split(a)
    bh, bl = _split(b)
    d = functools.partial(jnp.dot, preferred_element_type=F32)
    return d(ah, bh) + d(ah, bl) + d(al, bh)


def _mm_exact_rhs(a, b_bf16):
    d = functools.partial(jnp.dot, preferred_element_type=F32)
    a0 = a.astype(BF16)
    r1 = a - a0.astype(F32)
    a1 = r1.astype(BF16)
    a2 = (r1 - a1.astype(F32)).astype(BF16)
    return d(a0, b_bf16) + d(a1, b_bf16) + d(a2, b_bf16)


def _mm_exact_lhs(a_bf16, b):
    d = functools.partial(jnp.dot, preferred_element_type=F32)
    b0 = b.astype(BF16)
    r1 = b - b0.astype(F32)
    b1 = r1.astype(BF16)
    b2 = (r1 - b1.astype(F32)).astype(BF16)
    return d(a_bf16, b0) + d(a_bf16, b1) + d(a_bf16, b2)


def _group_matrix(value):
    i = lax.broadcasted_iota(jnp.int32, (BR_W, BR_W), 0) // HEAD_DIM
    j = lax.broadcasted_iota(jnp.int32, (BR_W, BR_W), 1) // HEAD_DIM
    return jnp.where(i == j, value, 0.0).astype(BF16)


def _head_of_lane(shape):
    return lax.broadcasted_iota(jnp.int32, shape, len(shape) - 1) // HEAD_DIM


def _norm_mod(x, g, shift, scale):
    ms = jnp.mean(x * x, axis=-1, keepdims=True)
    return (x * lax.rsqrt(ms + EPS) * g) * (1.0 + scale) + shift


def _mod_kernel(cs_ref, w_ref, b_ref, o_ref):
    o_ref[0] = _mm3(_silu(cs_ref[...]), w_ref[0]) + b_ref[0]


def _modulation(cs, ada_w, ada_b):
    depth, d, d3 = ada_w.shape
    rows = cs.shape[0]
    tn = 768
    return pl.pallas_call(
        _mod_kernel,
        grid=(depth, d3 // tn),
        in_specs=[pl.BlockSpec((rows, d), lambda l, j: (0, 0)),
                  pl.BlockSpec((1, d, tn), lambda l, j: (l, 0, j)),
                  pl.BlockSpec((1, 1, tn), lambda l, j: (l, 0, j))],
        out_specs=pl.BlockSpec((1, rows, tn), lambda l, j: (l, 0, j)),
        out_shape=jax.ShapeDtypeStruct((depth, rows, d3), F32),
        compiler_params=_params("arbitrary", "arbitrary"),
        name="adaln_modulation",
    )(cs, ada_w, ada_b.reshape(depth, 1, d3))


def _layer_spec(tail, layer):
    zeros = (0,) * len(tail)
    return pl.BlockSpec((1,) + tuple(tail), lambda i, j: (layer,) + zeros)


def _gmlp_mix(z, lng, lnb, ws_ref, bs, o_ref, row0):
    tm = z.shape[0]
    u = _gelu(z[:, 0:BR_W])
    v = _gelu(z[:, BR_W:2 * BR_W])
    gate = z[:, 2 * BR_W:3 * BR_W]
    avg = _group_matrix(1.0 / HEAD_DIM)
    mu = _mm_exact_rhs(v, avg)
    vc = v - mu
    var = _mm_exact_rhs(vc * vc, avg)
    vh = (vc * lax.rsqrt(var + EPS) * lng + lnb).astype(BF16)
    head = _head_of_lane((CHUNK, BR_W))
    su = u * _silu(gate)
    for c in range(tm // CHUNK):
        rows = slice(c * CHUNK, (c + 1) * CHUNK)
        vch = vh[rows]
        mixed = jnp.zeros((CHUNK, BR_W), F32)
        for g in range(N_HEADS):
            mg = jnp.dot(ws_ref[0, g], vch, preferred_element_type=F32)
            mixed = jnp.where(head == g, mg, mixed)
        o_ref[0, row0 + c * CHUNK:row0 + (c + 1) * CHUNK, :] = (su[rows] * (mixed + bs)).astype(o_ref.dtype)


PREP_R, PREP_V, PREP_A = 0, 1, 2
PREP_DIR = 3
PREP_BONUS = 9
PREP_COLS = 10 * BR_W
DECAY_SCALE = 0.6065306597126334


def _token_shift(z, halo, mu):
    tm = z.shape[0]
    n = tm + 8
    ze = jnp.concatenate([z, halo], axis=0)
    neighbours = pltpu.roll(ze, 1, 0)[0:tm] + pltpu.roll(ze, n - 1, 0)[0:tm]
    return z * (1.0 - mu) + neighbours * (0.5 * mu)


def _rwkv_prepare(zf, kkw, ka, rk, w0, a0, w2_ref, a2_ref, o_ref, rows):
    r = zf[:, 0:BR_W]
    k = zf[:, BR_W:2 * BR_W]
    v = zf[:, 2 * BR_W:3 * BR_W]
    lo = zf[:, 3 * BR_W:3 * BR_W + 2 * LORA]
    ones = _group_matrix(1.0)
    kk = k * kkw
    kk = kk / jnp.maximum(jnp.sqrt(_mm_exact_rhs(kk * kk, ones)), 1e-12)
    o_ref[0, rows, PREP_R * BR_W:(PREP_R + 1) * BR_W] = r
    o_ref[0, rows, PREP_V * BR_W:(PREP_V + 1) * BR_W] = v
    o_ref[0, rows, PREP_A * BR_W:(PREP_A + 1) * BR_W] = -kk
    tlo = jnp.tanh(lo)
    kd_sum = jnp.zeros_like(k)
    for d in range(2):
        y = w0[d:d + 1, :] + _mm3(tlo, w2_ref[0, d])
        a = _sigmoid(a0[d:d + 1, :] + _mm3(lo, a2_ref[0, d]))
        kd = k * (1.0 + (a - 1.0) * ka)
        c0 = (PREP_DIR + 3 * d) * BR_W
        o_ref[0, rows, c0:c0 + BR_W] = _sigmoid(y) * (-DECAY_SCALE)
        o_ref[0, rows, c0 + BR_W:c0 + 2 * BR_W] = kd
        o_ref[0, rows, c0 + 2 * BR_W:c0 + 3 * BR_W] = kk * a
        kd_sum = kd_sum + kd
    o_ref[0, rows, PREP_BONUS * BR_W:] = _mm_exact_rhs(r * kd_sum * rk, ones) * v


OFF_GM = 0
OFF_NA = OFF_GM + SEG_GM
OFF_RWIN = OFF_NA + SEG_NA
OFF_RWG = OFF_RWIN + SEG_RWIN
OFF_PL = OFF_RWG + SEG_RWG


def _projmix_kernel(x_ref, xp_ref, xn_ref, ng_ref, sh_ref, sc_ref, w_ref,
                    lng_ref, lnb_ref, ws_ref, bs_ref,
                    mu_ref, kkw_ref, ka_ref, rk_ref, w0_ref, a0_ref, w2_ref, a2_ref,
                    gm_ref, na_ref, prep_ref, rwg_ref, pl_ref):
    j = pl.program_id(1)
    g, sh, sc = ng_ref[0], sh_ref[0, 0], sc_ref[0, 0]
    h = _norm_mod(x_ref[0], g, sh, sc).astype(BF16)
    proj = lambda lhs, off, width: jnp.dot(lhs, w_ref[0, :, off:off + width], preferred_element_type=F32)

    xh = jnp.concatenate([xn_ref[0], xp_ref[0]], axis=0)
    zh = proj(_norm_mod(xh, g, sh, sc).astype(BF16), OFF_RWIN, SEG_RWIN)
    row = lax.broadcasted_iota(jnp.int32, (8, SEG_RWIN), 0)
    has_prev = jnp.where(j > 0, 1.0, 0.0)
    has_next = jnp.where(j < pl.num_programs(1) - 1, 1.0, 0.0)
    halo = jnp.where(row == 0, zh[0:8], 0.0) * has_next + jnp.where(row == 7, zh[8:16], 0.0) * has_prev
    zf = _token_shift(proj(h, OFF_RWIN, SEG_RWIN), halo, mu_ref[0])

    _gmlp_mix(proj(h, OFF_GM, SEG_GM), lng_ref[0], lnb_ref[0], ws_ref, bs_ref[0], gm_ref, 0)
    na_ref[0] = proj(h, OFF_NA, SEG_NA).astype(na_ref.dtype)
    rwg_ref[0] = proj(h, OFF_RWG, SEG_RWG).astype(rwg_ref.dtype)
    pl_ref[0] = proj(h, OFF_PL, SEG_PL).astype(pl_ref.dtype)
    _rwkv_prepare(zf, kkw_ref[0], ka_ref[0], rk_ref[0], w0_ref[0], a0_ref[0], w2_ref, a2_ref,
                  prep_ref, slice(None))


def _project_mix(x, layer, mod, mod_row, p):
    b, t, d = x.shape
    tm = min(t, 256)
    nblk8 = t // 8
    row = lambda i, j: (i, j, 0)
    modspec = lambda c: pl.BlockSpec((1, 1, 1, d), lambda i, j: (layer, mod_row(i), 0, c))
    widths = (BR_W, SEG_NA, PREP_COLS, SEG_RWG, SEG_PL)
    dtypes = (BF16, BF16, F32, BF16, BF16)
    return pl.pallas_call(
        _projmix_kernel,
        grid=(b, t // tm),
        in_specs=[pl.BlockSpec((1, tm, d), row),
                  pl.BlockSpec((1, 8, d), lambda i, j: (i, jnp.maximum(j * (tm // 8) - 1, 0), 0)),
                  pl.BlockSpec((1, 8, d), lambda i, j: (i, jnp.minimum((j + 1) * (tm // 8), nblk8 - 1), 0)),
                  _layer_spec((1, d), layer), modspec(0), modspec(1),
                  _layer_spec((d, MIX_COLS), layer),
                  _layer_spec((1, BR_W), layer), _layer_spec((1, BR_W), layer),
                  _layer_spec((N_HEADS, CHUNK, CHUNK), layer), _layer_spec((CHUNK, BR_W), layer),
                  _layer_spec((1, SEG_RWIN), layer), _layer_spec((1, BR_W), layer),
                  _layer_spec((1, BR_W), layer), _layer_spec((1, BR_W), layer),
                  _layer_spec((2, BR_W), layer), _layer_spec((2, BR_W), layer),
                  _layer_spec((2, 2 * LORA, BR_W), layer), _layer_spec((2, 2 * LORA, BR_W), layer)],
        out_specs=[pl.BlockSpec((1, tm, w), row) for w in widths],
        out_shape=[jax.ShapeDtypeStruct((b, t, w), dt) for w, dt in zip(widths, dtypes)],
        compiler_params=_params("parallel", "parallel"),
        name="norm_in_proj_mix",
    )(x, x, x, p['norm_g'], mod, mod, p['w_mix'],
      p['gm_ln_g'], p['gm_ln_b'], p['gm_ws'], p['gm_bs'],
      p['rw_mu'], p['rw_kk'], p['rw_ka'], p['rw_rk'], p['rw_w0'], p['rw_a0'], p['rw_w2'], p['rw_a2'])


def _pool_kernel(p_ref, g_ref, w_ref, sc_ref, o_ref, ext_ref, *, seq_len):
    j = pl.program_id(1)
    tm = o_ref.shape[1]
    n = tm + 2 * POOL_HALO

    @pl.when(j == 0)
    def _():
        ext_ref[0:POOL_HALO, :] = jnp.zeros((POOL_HALO, BR_W), F32)
        ext_ref[POOL_HALO + seq_len:, :] = jnp.zeros((POOL_HALO, BR_W), F32)
        ext_ref[POOL_HALO:POOL_HALO + seq_len, :] = p_ref[0].astype(F32)

    t0 = pl.multiple_of(j * tm, 8)
    x = ext_ref[pl.ds(t0, n), :]
    s2 = x + pltpu.roll(x, 1, 0)
    s4 = pltpu.roll(s2, 1, 0) + pltpu.roll(s2, n - 1, 0)
    s8 = pltpu.roll(s4, 2, 0) + pltpu.roll(s4, n - 2, 0)
    s16 = pltpu.roll(s8, 4, 0) + pltpu.roll(s8, n - 4, 0)
    mid = slice(POOL_HALO, POOL_HALO + tm)
    grp = _head_of_lane((tm, BR_W))
    ssum = jnp.where(grp == 0, s2[mid], jnp.where(grp == 1, s4[mid], jnp.where(grp == 2, s8[mid], s16[mid])))
    half = jnp.where(grp == 0, 1, jnp.where(grp == 1, 2, jnp.where(grp == 2, 4, 8)))
    t = t0 + lax.broadcasted_iota(jnp.int32, (tm, BR_W), 0)
    cnt = jnp.minimum(t + half, seq_len) - jnp.maximum(t - half, 0)
    d = ssum / cnt.astype(F32) - x[mid]
    y = _mm(d, w_ref[0])
    o_ref[0] = (y * sc_ref[0] * _silu(g_ref[0].astype(F32))).astype(o_ref.dtype)


def _pool(plz, layer, p):
    b, t, _ = plz.shape
    tm = min(t, 512)
    return pl.pallas_call(
        functools.partial(_pool_kernel, seq_len=t),
        grid=(b, t // tm),
        in_specs=[pl.BlockSpec((1, t, BR_W), lambda i, j: (i, 0, 0)),
                  pl.BlockSpec((1, tm, BR_W), lambda i, j: (i, j, 1)),
                  _layer_spec((BR_W, BR_W), layer),
                  _layer_spec((1, BR_W), layer)],
        out_specs=pl.BlockSpec((1, tm, BR_W), lambda i, j: (i, j, 0)),
        out_shape=jax.ShapeDtypeStruct((b, t, BR_W), BF16),
        scratch_shapes=[pltpu.VMEM((t + 2 * POOL_HALO, BR_W), F32)],
        compiler_params=_params("parallel", "arbitrary"),
        name="pool_mixer",
    )(plz, plz, p['pl_w'], p['pl_scale'])


def _stack_heads(q):
    head = _head_of_lane(q.shape)
    return jnp.concatenate([jnp.where(head == h, q, 0.0) for h in range(N_HEADS)], axis=0)


def _unstack_heads(o, n):
    head = _head_of_lane((n, BR_W))
    out = jnp.zeros((n, BR_W), F32)
    for h in range(N_HEADS):
        out = jnp.where(head == h, o[h * n:(h + 1) * n], out)
    return out


def _natten_kernel(q_ref, k_ref, v_ref, g_ref, kc_ref, vc_ref, bias_ref, o_ref, *, rows):
    band = NA_ROWS * GRID_W
    rng = range(NA_RT)
    kc = kc_ref[0]
    vc = vc_ref[0]
    r = [pl.program_id(1) * NA_RT + i for i in rng]
    rs = [jnp.clip(r[i] - NA_ROWS // 2, 0, rows - NA_ROWS) for i in rng]
    start = [pl.multiple_of(rs[i] * GRID_W, GRID_W) for i in rng]
    qs = [_stack_heads(q_ref[0, i * GRID_W:(i + 1) * GRID_W, :] * (HEAD_DIM ** -0.5)) for i in rng]
    kb = [k_ref[0, pl.ds(start[i], band), :] for i in rng]
    vb = [v_ref[0, pl.ds(start[i], band), :] for i in rng]
    s_b = [_mm_nt(qs[i], kb[i]) + bias_ref[0, rs[i] - r[i] + NA_ROWS - 1] for i in rng]
    s_c = [_mm_nt(qs[i], kc) for i in rng]
    m = [jnp.maximum(jnp.max(s_b[i], axis=1, keepdims=True), jnp.max(s_c[i], axis=1, keepdims=True)) for i in rng]
    e_b = [jnp.exp(s_b[i] - m[i]) for i in rng]
    e_c = [jnp.exp(s_c[i] - m[i]) for i in rng]
    den = [jnp.sum(e_b[i], axis=1, keepdims=True) + jnp.sum(e_c[i], axis=1, keepdims=True) for i in rng]
    pv = [(_mm(e_b[i], vb[i]) + _mm(e_c[i], vc)) / den[i] for i in rng]
    for i in rng:
        rows_i = slice(i * GRID_W, (i + 1) * GRID_W)
        gate = _silu(g_ref[0, rows_i, :].astype(F32))
        o_ref[0, rows_i, :] = (_unstack_heads(pv[i], GRID_W) * gate).astype(o_ref.dtype)


def _natten_bias(rpb):
    depth = rpb.shape[0]
    c = np.arange(GRID_W)
    c_start = np.clip(c - NA_COLS // 2, 0, GRID_W - NA_COLS)
    col_ok = (c[None, :] >= c_start[:, None]) & (c[None, :] < c_start[:, None] + NA_COLS)
    dc = np.clip(c[None, :] - c[:, None], -(NA_COLS - 1), NA_COLS - 1) + NA_COLS - 1
    select = (np.arange(2 * NA_COLS - 1)[:, None, None] == dc[None]).astype(np.float32)
    rows = jnp.stack([rpb[:, :, o:o + NA_ROWS] for o in range(NA_ROWS)], axis=1)
    tab = jnp.einsum('lohic,cqk->lohqik', rows, jnp.asarray(select), precision=lax.Precision.HIGHEST)
    tab = jnp.where(jnp.asarray(col_ok)[None, None, None, :, None, :], tab, NEG)
    return tab.reshape(depth, NA_ROWS, N_HEADS * GRID_W, NA_ROWS * GRID_W)


def _natten(na, na_ctx, layer, bias):
    b, t, _ = na.shape
    lc = na_ctx.shape[1]
    rows = t // GRID_W
    band = NA_ROWS * GRID_W
    tq = NA_RT * GRID_W
    qrow = lambda c: (lambda i, r: (i, r, c))
    full = lambda c: (lambda i, r: (i, 0, c))
    return pl.pallas_call(
        functools.partial(_natten_kernel, rows=rows),
        grid=(b, rows // NA_RT),
        in_specs=[pl.BlockSpec((1, tq, BR_W), qrow(0)),
                  pl.BlockSpec((1, t, BR_W), full(1)),
                  pl.BlockSpec((1, t, BR_W), full(2)),
                  pl.BlockSpec((1, tq, BR_W), qrow(3)),
                  pl.BlockSpec((1, lc, BR_W), full(1)),
                  pl.BlockSpec((1, lc, BR_W), full(2)),
                  _layer_spec((NA_ROWS, N_HEADS * GRID_W, band), layer)],
        out_specs=pl.BlockSpec((1, tq, BR_W), qrow(0)),
        out_shape=jax.ShapeDtypeStruct((b, t, BR_W), BF16),
        compiler_params=_params("parallel", "arbitrary"),
        name="natten_mixer",
    )(na, na, na, na, na_ctx, na_ctx, bias)


def _ctx_attn_kernel(q_ref, k_ref, v_ref, g_ref, o_ref):
    n = q_ref.shape[1]
    qs = _stack_heads(q_ref[0] * (HEAD_DIM ** -0.5))
    s = _mm_nt(qs, k_ref[0])
    e = jnp.exp(s - jnp.max(s, axis=1, keepdims=True))
    pv = _mm(e, v_ref[0]) / jnp.sum(e, axis=1, keepdims=True)
    o_ref[0] = (_unstack_heads(pv, n) * _silu(g_ref[0].astype(F32))).astype(o_ref.dtype)


def _ctx_attention(na_ctx):
    b, lc, _ = na_ctx.shape
    tq = 64
    qrow = lambda c: (lambda i, j: (i, j, c))
    full = lambda c: (lambda i, j: (i, 0, c))
    return pl.pallas_call(
        _ctx_attn_kernel,
        grid=(b, lc // tq),
        in_specs=[pl.BlockSpec((1, tq, BR_W), qrow(0)),
                  pl.BlockSpec((1, lc, BR_W), full(1)),
                  pl.BlockSpec((1, lc, BR_W), full(2)),
                  pl.BlockSpec((1, tq, BR_W), qrow(3))],
        out_specs=pl.BlockSpec((1, tq, BR_W), qrow(0)),
        out_shape=jax.ShapeDtypeStruct((b, lc, BR_W), BF16),
        compiler_params=_params("parallel", "arbitrary"),
        name="ctx_attention",
    )(na_ctx, na_ctx, na_ctx, na_ctx)


def _stack_pair(x):
    head = _head_of_lane(x.shape)
    return jnp.concatenate([jnp.where(head == 0, x, 0.0), jnp.where(head == 1, x, 0.0)], axis=0)


def _scan_chunks(chains):
    n = chains[0][0].shape[0]
    m = 2 * n
    nc = len(chains)
    rng = range(nc)
    rev = [c[7] for c in chains]
    ti = lax.broadcasted_iota(jnp.int32, (n, n), 0)
    si = lax.broadcasted_iota(jnp.int32, (n, n), 1)
    tri = {False: jnp.where(si <= ti, 1.0, 0.0).astype(BF16), True: jnp.where(si >= ti, 1.0, 0.0).astype(BF16)}
    tri3 = {k: jnp.concatenate([t, t, t], axis=1) for k, t in tri.items()}
    i2 = lax.broadcasted_iota(jnp.int32, (m, 2 * m), 0)
    j2 = lax.broadcasted_iota(jnp.int32, (m, 2 * m), 1) % n
    t2 = i2 % n
    same = jnp.where(i2 < n, 0, 1)
    keep = {False: j2 < t2 + same, True: j2 > t2 - same}
    ri = lax.broadcasted_iota(jnp.int32, (LANES, LANES), 0)
    ci = lax.broadcasted_iota(jnp.int32, (LANES, LANES), 1)
    eye = ri == ci
    same_head = (ri // HEAD_DIM) == (ci // HEAD_DIM)
    d = functools.partial(jnp.dot, preferred_element_type=F32)

    def pieces3(x):
        x0 = x.astype(BF16)
        r1 = x - x0.astype(F32)
        x1 = r1.astype(BF16)
        return jnp.concatenate([x0, x1, (r1 - x1.astype(F32)).astype(BF16)], axis=0)

    cum = [d(tri3[rev[c]], pieces3(chains[c][3])) for c in rng]
    tot = [jnp.sum(chains[c][3], axis=0, keepdims=True) for c in rng]
    e_neg = [jnp.exp(-cum[c]) for c in rng]
    ar = [jnp.concatenate([chains[c][2] * jnp.exp(cum[c] - chains[c][3]),
                           chains[c][0] * jnp.exp(cum[c])], axis=0).astype(BF16) for c in rng]
    bk_st = [jnp.concatenate([_stack_pair((chains[c][5] * e_neg[c]).astype(BF16)),
                              _stack_pair((chains[c][4] * e_neg[c]).astype(BF16))], axis=0) for c in rng]
    v_bf = [chains[c][1].astype(BF16) for c in rng]

    g = [jnp.where(keep[rev[c]], _mm_nt(ar[c], bk_st[c]), 0.0) for c in rng]
    lhs = [jnp.concatenate([ar[c], g[c][:, m:].astype(BF16)], axis=1) for c in rng]
    rhs = [jnp.concatenate([chains[c][6].astype(BF16), _stack_pair(v_bf[c])], axis=0) for c in rng]
    xy = [d(lhs[c], rhs[c]) for c in rng]

    u = [xy[c][:n] for c in rng]
    p = [g[c][:n, :m] for c in rng]
    step = 1
    while step < n:
        step *= 2
        last = step >= n
        nxt_u, nxt_p = [], []
        for c in rng:
            p_bf = p[c].astype(BF16)
            u_hi, u_lo = _split(u[c])
            rhs_parts = [_stack_pair(u_hi), _stack_pair(u_lo)] + ([] if last else [_stack_pair(p_bf)])
            acc = d(p_bf, jnp.concatenate(rhs_parts, axis=1))
            nxt_u.append(u[c] + (acc[:, :LANES] + acc[:, LANES:2 * LANES]))
            nxt_p.append(None if last else acc[:, 2 * LANES:])
        u, p = nxt_u, nxt_p

    outs = []
    for c in rng:
        u_bf = u[c].astype(BF16)
        y = xy[c][n:] + d(g[c][n:, :m].astype(BF16), _stack_pair(u_bf))
        e_rest = jnp.exp(tot[c] - cum[c])
        bk_rest = jnp.concatenate([chains[c][5] * e_rest, chains[c][4] * e_rest], axis=0)
        decay = jnp.sum(jnp.where(eye, jnp.exp(tot[c]), 0.0), axis=1, keepdims=True)
        upd = _mm_tn(bk_rest, jnp.concatenate([u_bf, v_bf[c]], axis=0))
        outs.append((y, chains[c][6] * decay + jnp.where(same_head, upd, 0.0)))
    return outs


def _scan_kernel(sf_ref, df_ref, sb_ref, db_ref, s0_ref, yf_ref, yb_ref, sfin_ref, state_ref):
    j = pl.program_id(1)

    @pl.when(j == 0)
    def _():
        state_ref[...] = s0_ref[...]

    nb = sf_ref.shape[0]
    npair = BR_W // LANES
    chains = []
    for i in range(nb):
        for dr, (s_ref, d_ref) in enumerate(((sf_ref, df_ref), (sb_ref, db_ref))):
            for p in range(npair):
                col = lambda blk: slice(blk * BR_W + p * LANES, blk * BR_W + (p + 1) * LANES)
                chains.append((s_ref[i, :, col(PREP_R)], s_ref[i, :, col(PREP_V)], s_ref[i, :, col(PREP_A)],
                               d_ref[i, :, col(0)], d_ref[i, :, col(1)], d_ref[i, :, col(2)],
                               state_ref[i, dr, p], dr == 1))
    outs = iter(_scan_chunks(chains))
    for i in range(nb):
        for dr, y_ref in enumerate((yf_ref, yb_ref)):
            for p in range(npair):
                y, new_state = next(outs)
                y_ref[i, :, p * LANES:(p + 1) * LANES] = y
                state_ref[i, dr, p] = new_state

    @pl.when(j == pl.num_programs(1) - 1)
    def _():
        sfin_ref[...] = state_ref[...]


def _rwkv_scan(prep, s0):
    b, t, _ = prep.shape
    n = t // SCAN_L
    npair = BR_W // LANES
    blk3 = 3 * BR_W
    fwd = lambda c: (lambda i, j: (i, j, c))
    bwd = lambda c: (lambda i, j: (i, n - 1 - j, c))
    st = lambda i, j: (i, 0, 0, 0, 0)
    nb = SCAN_NB if b % SCAN_NB == 0 else 1
    return pl.pallas_call(
        _scan_kernel,
        grid=(b // nb, n),
        in_specs=[pl.BlockSpec((nb, SCAN_L, blk3), fwd(0)),
                  pl.BlockSpec((nb, SCAN_L, blk3), fwd(1)),
                  pl.BlockSpec((nb, SCAN_L, blk3), bwd(0)),
                  pl.BlockSpec((nb, SCAN_L, blk3), bwd(2)),
                  pl.BlockSpec((nb, 2, npair, LANES, LANES), st)],
        out_specs=[pl.BlockSpec((nb, SCAN_L, BR_W), fwd(0)),
                   pl.BlockSpec((nb, SCAN_L, BR_W), bwd(0)),
                   pl.BlockSpec((nb, 2, npair, LANES, LANES), st)],
        out_shape=[jax.ShapeDtypeStruct((b, t, BR_W), F32),
                   jax.ShapeDtypeStruct((b, t, BR_W), F32),
                   jax.ShapeDtypeStruct((b, 2, npair, LANES, LANES), F32)],
        scratch_shapes=[pltpu.VMEM((nb, 2, npair, LANES, LANES), F32)],
        compiler_params=_params("parallel", "arbitrary"),
        name="rwkv_scan",
    )(prep, prep, prep, prep, s0)


def _rwkv_finish(y_sum, bonus, gate, gn_g, gn_b):
    avg = _group_matrix(1.0 / HEAD_DIM)
    mu = _mm_exact_rhs(y_sum, avg)
    oc = y_sum - mu
    var = _mm_exact_rhs(oc * oc, avg)
    o = oc * lax.rsqrt(var + GN_EPS) * gn_g + gn_b + bonus
    return o * _silu(gate)


def _merge_kernel(x_ref, g_ref, sh_ref, sc_ref, gt_ref, gm_ref, na_ref, yf_ref, yb_ref, bonus_ref, rwg_ref,
                  gng_ref, gnb_ref, pl_ref, wm_ref, wbr_ref, wout_ref, fin_ref, o_ref, *, final):
    x = x_ref[0]
    d = x.shape[-1]
    h = _norm_mod(x, g_ref[0], sh_ref[0, 0], sc_ref[0, 0]).astype(BF16)
    rw = _rwkv_finish(yf_ref[0] + yb_ref[0], bonus_ref[0], rwg_ref[0].astype(F32), gng_ref[0], gnb_ref[0])
    y = None
    for i, br in enumerate((gm_ref[0], na_ref[0], rw.astype(BF16), pl_ref[0])):
        logits = jnp.dot(h, wm_ref[0, :, i * d:(i + 1) * d], preferred_element_type=F32)
        term = _sigmoid(logits) * jnp.dot(br, wbr_ref[0, i], preferred_element_type=F32)
        y = term if y is None else y + term
    out = x + gt_ref[0, 0] * jnp.dot(y.astype(BF16), wout_ref[0], preferred_element_type=F32)
    if final:
        ms = jnp.mean(out * out, axis=-1, keepdims=True)
        out = out * lax.rsqrt(ms + EPS) * fin_ref[...]
    o_ref[0] = out


def _merge(x, layer, mod, mod_row, gm, na, yf, yb, prep, rwg, plb, p, final):
    b, t, d = x.shape
    tm = min(t, 256)
    row = lambda i, j: (i, j, 0)
    modspec = lambda c: pl.BlockSpec((1, 1, 1, d), lambda i, j: (layer, mod_row(i), 0, c))
    br = pl.BlockSpec((1, tm, BR_W), row)
    return pl.pallas_call(
        functools.partial(_merge_kernel, final=final),
        grid=(b, t // tm),
        in_specs=[pl.BlockSpec((1, tm, d), row), _layer_spec((1, d), layer), modspec(0), modspec(1), modspec(2),
                  br, br, br, br, pl.BlockSpec((1, tm, BR_W), lambda i, j: (i, j, PREP_BONUS)), br,
                  _layer_spec((1, BR_W), layer), _layer_spec((1, BR_W), layer), br,
                  _layer_spec((d, N_BRANCH * d), layer), _layer_spec((N_BRANCH, BR_W, d), layer),
                  _layer_spec((d, d), layer), pl.BlockSpec((1, d), lambda i, j: (0, 0))],
        out_specs=pl.BlockSpec((1, tm, d), row),
        out_shape=jax.ShapeDtypeStruct((b, t, d), F32),
        compiler_params=_params("parallel", "parallel"),
        name="merge_out_proj",
    )(x, p['norm_g'], mod, mod, mod, gm, na, yf, yb, prep, rwg, p['rw_gn_g'], p['rw_gn_b'], plb,
      p['w_merge'], p['w_br'], p['w_out'], p['final_g'])


def _layer(x, ctx, layer, mod, p, last):
    b = x.shape[0]
    latent_row = lambda i: i
    ctx_row = lambda i: b

    gm_c, na_c, prep_c, rwg_c, pl_c = _project_mix(ctx, layer, mod, ctx_row, p)
    gm_x, na_x, prep_x, rwg_x, pl_x = _project_mix(x, layer, mod, latent_row, p)

    zero_state = jnp.zeros((b, 2, BR_W // LANES, LANES, LANES), F32)
    yf_c, yb_c, s_ctx = _rwkv_scan(prep_c, zero_state)
    yf_x, yb_x, _ = _rwkv_scan(prep_x, s_ctx)

    x_new = _merge(x, layer, mod, latent_row, gm_x, _natten(na_x, na_c, layer, p['na_bias']),
                   yf_x, yb_x, prep_x, rwg_x, _pool(pl_x, layer, p), p, final=last)
    if last:
        return x_new, ctx
    ctx_new = _merge(ctx, layer, mod, ctx_row, gm_c, _ctx_attention(na_c),
                     yf_c, yb_c, prep_c, rwg_c, _pool(pl_c, layer, p), p, final=False)
    return x_new, ctx_new


def kernel(x, c, ctx, c_ctx, ada_w, ada_b, norm_g, w_in, gm_ln_g, gm_ln_b, gm_ws, gm_bs, na_rpb, rw_mu, rw_w0, rw_w2, rw_a0, rw_a2, rw_kk, rw_ka, rw_rk, rw_gn_g, rw_gn_b, pl_w, pl_scale, w_br, w_out, final_g):
    b, _, d = x.shape
    depth = ada_w.shape[0]
    mod_rows = 16
    cs = jnp.zeros((mod_rows, d), F32).at[:b].set(c).at[b].set(c_ctx)
    mod = _modulation(cs, ada_w, ada_b).reshape(depth, mod_rows, 1, 3 * d)

    vec = lambda a: a.reshape(depth, 1, -1)
    w_bf = w_in.astype(BF16)
    lora_zeros = jnp.zeros((depth, 2, LORA, BR_W), F32)
    p = {
        'norm_g': vec(norm_g), 'final_g': final_g.reshape(1, d),
        'w_mix': w_bf[:, :, :MIX_COLS], 'w_merge': w_bf[:, :, MIX_COLS:],
        'w_br': w_br.astype(BF16), 'w_out': w_out.astype(BF16),
        'gm_ln_g': vec(gm_ln_g), 'gm_ln_b': vec(gm_ln_b), 'gm_ws': gm_ws.astype(BF16),
        'gm_bs': jnp.repeat(gm_bs.transpose(0, 2, 1), HEAD_DIM, axis=2),
        'na_bias': _natten_bias(na_rpb),
        'rw_mu': vec(rw_mu), 'rw_kk': vec(rw_kk), 'rw_ka': vec(rw_ka), 'rw_rk': vec(rw_rk),
        'rw_w0': rw_w0, 'rw_a0': rw_a0,
        'rw_w2': jnp.concatenate([rw_w2, lora_zeros], axis=2),
        'rw_a2': jnp.concatenate([lora_zeros, rw_a2], axis=2),
        'rw_gn_g': vec(rw_gn_g), 'rw_gn_b': vec(rw_gn_b),
        'pl_w': jnp.einsum('lgcd,gh->lgchd', pl_w, jnp.eye(N_HEADS, dtype=F32)).reshape(depth, BR_W, BR_W).astype(BF16),
        'pl_scale': vec(pl_scale),
    }
    for layer in range(depth):
        x, ctx = _layer(x, ctx, layer, mod, p, last=(layer == depth - 1))
    return x
```

```python
import functools

import jax
import jax.numpy as jnp
import numpy as np
from jax import lax
from jax.experimental import pallas as pl
from jax.experimental.pallas import tpu as pltpu

F32 = jnp.float32
BF16 = jnp.bfloat16

N_HEADS = 4
N_BRANCH = 4
HEAD_DIM = 64
BR_W = N_HEADS * HEAD_DIM
LANES = 128
GRID_W = 64
NA_ROWS = 8
NA_COLS = 16
NA_RT = 4
CHUNK = 128
LORA = 64
POOL_WINDOWS = (2, 4, 8, 16)
POOL_HALO = 16
SCAN_L = 64
SCAN_NB = 4
EPS = 1e-6
GN_EPS = 64e-5
NEG = -1e30
VMEM_LIMIT = 56 * 1024 * 1024

SEG_GM, SEG_NA, SEG_RWIN, SEG_RWG, SEG_PL = 3 * BR_W, 4 * BR_W, 3 * BR_W + 2 * LORA, BR_W, 2 * BR_W
MIX_COLS = SEG_GM + SEG_NA + SEG_RWIN + SEG_RWG + SEG_PL


def _params(*sem):
    return pltpu.CompilerParams(dimension_semantics=sem, vmem_limit_bytes=VMEM_LIMIT)


def _sigmoid(x):
    return 1.0 / (1.0 + jnp.exp(-x))


def _sigmoid_t(x):
    return 0.5 * jnp.tanh(0.5 * x) + 0.5


def _silu(x):
    return x * _sigmoid_t(x)


def _gelu(x):
    return 0.5 * x * (1.0 + jnp.tanh(0.7978845608028654 * (x + 0.044715 * x * x * x)))


def _mm(a, b):
    return jnp.dot(a.astype(BF16), b.astype(BF16), preferred_element_type=F32)


def _mm_nt(a, b):
    return lax.dot_general(a.astype(BF16), b.astype(BF16), (((1,), (1,)), ((), ())),
                           preferred_element_type=F32)


def _mm_tn(a, b):
    return lax.dot_general(a.astype(BF16), b.astype(BF16), (((0,), (0,)), ((), ())),
                           preferred_element_type=F32)


def _split(x):
    hi = x.astype(BF16)
    lo = (x - hi.astype(F32)).astype(BF16)
    return hi, lo


def _mm3(a, b):
    ah, al = _split(a)
    bh, bl = _split(b)
    d = functools.partial(jnp.dot, preferred_element_type=F32)
    return d(ah, bh) + d(ah, bl) + d(al, bh)


def _mm_exact_rhs(a, b_bf16, precise=False):
    d = functools.partial(jnp.dot, preferred_element_type=F32)
    if not precise:
        return d(a.astype(BF16), b_bf16)
    a_hi, a_lo = _split(a)
    return d(a_hi, b_bf16) + d(a_lo, b_bf16)


def _group_matrix(value):
    i = lax.broadcasted_iota(jnp.int32, (BR_W, BR_W), 0) // HEAD_DIM
    j = lax.broadcasted_iota(jnp.int32, (BR_W, BR_W), 1) // HEAD_DIM
    return jnp.where(i == j, value, 0.0).astype(BF16)


def _head_of_lane(shape):
    return lax.broadcasted_iota(jnp.int32, shape, len(shape) - 1) // HEAD_DIM


def _norm_mod(x, g, shift, scale):
    ms = jnp.mean(x * x, axis=-1, keepdims=True)
    return (x * lax.rsqrt(ms + EPS) * g) * (1.0 + scale) + shift


def _mod_kernel(cs_ref, w_ref, b_ref, o_ref):
    o_ref[0] = _mm3(_silu(cs_ref[...]), w_ref[0]) + b_ref[0]


def _modulation(cs, ada_w, ada_b):
    depth, d, d3 = ada_w.shape
    rows = cs.shape[0]
    tn = 768
    return pl.pallas_call(
        _mod_kernel,
        grid=(depth, d3 // tn),
        in_specs=[pl.BlockSpec((rows, d), lambda l, j: (0, 0)),
                  pl.BlockSpec((1, d, tn), lambda l, j: (l, 0, j)),
                  pl.BlockSpec((1, 1, tn), lambda l, j: (l, 0, j))],
        out_specs=pl.BlockSpec((1, rows, tn), lambda l, j: (l, 0, j)),
        out_shape=jax.ShapeDtypeStruct((depth, rows, d3), F32),
        compiler_params=_params("arbitrary", "arbitrary"),
        name="adaln_modulation",
    )(cs, ada_w, ada_b.reshape(depth, 1, d3))


def _layer_spec(tail, layer):
    zeros = (0,) * len(tail)
    return pl.BlockSpec((1,) + tuple(tail), lambda i, j: (layer,) + zeros)


def _gmlp_mix(z, lng, lnb, ws_ref, bs, o_ref, row0):
    tm = z.shape[0]
    u = _gelu(z[:, 0:BR_W])
    v = _gelu(z[:, BR_W:2 * BR_W])
    gate = z[:, 2 * BR_W:3 * BR_W]
    avg = _group_matrix(1.0 / HEAD_DIM)
    mu = _mm_exact_rhs(v, avg, precise=True)
    vc = v - mu
    var = _mm_exact_rhs(vc * vc, avg)
    vh = (vc * lax.rsqrt(var + EPS) * lng + lnb).astype(BF16)
    head = _head_of_lane((CHUNK, BR_W))
    su = u * _silu(gate)
    for c in range(tm // CHUNK):
        rows = slice(c * CHUNK, (c + 1) * CHUNK)
        vch = vh[rows]
        mixed = jnp.zeros((CHUNK, BR_W), F32)
        for g in range(N_HEADS):
            mg = jnp.dot(ws_ref[0, g], vch, preferred_element_type=F32)
            mixed = jnp.where(head == g, mg, mixed)
        o_ref[0, row0 + c * CHUNK:row0 + (c + 1) * CHUNK, :] = (su[rows] * (mixed + bs)).astype(o_ref.dtype)


PREP_R, PREP_V, PREP_A = 0, 1, 2
PREP_DIR = 3
PREP_BONUS = 9
PREP_COLS = 10 * BR_W
DECAY_SCALE = 0.6065306597126334


def _token_shift(z, halo, mu):
    tm = z.shape[0]
    n = tm + 8
    ze = jnp.concatenate([z, halo], axis=0)
    neighbours = pltpu.roll(ze, 1, 0)[0:tm] + pltpu.roll(ze, n - 1, 0)[0:tm]
    return z * (1.0 - mu) + neighbours * (0.5 * mu)


def _rwkv_prepare(zf, kkw, ka, rk, w0, a0, lora_ref, o_ref, rows):
    r = zf[:, 0:BR_W]
    k = zf[:, BR_W:2 * BR_W]
    v = zf[:, 2 * BR_W:3 * BR_W]
    lo = zf[:, 3 * BR_W:3 * BR_W + 2 * LORA]
    ones = _group_matrix(1.0)
    kk = k * kkw
    kk = kk / jnp.maximum(jnp.sqrt(_mm_exact_rhs(kk * kk, ones)), 1e-12)
    o_ref[0, rows, PREP_R * BR_W:(PREP_R + 1) * BR_W] = r
    o_ref[0, rows, PREP_V * BR_W:(PREP_V + 1) * BR_W] = v
    o_ref[0, rows, PREP_A * BR_W:(PREP_A + 1) * BR_W] = -kk
    decay_lane = lax.broadcasted_iota(jnp.int32, lo.shape, 1) < LORA
    t_hi, t_lo = _split(jnp.where(decay_lane, jnp.tanh(lo), lo))
    low_rank = jnp.dot(jnp.concatenate([t_hi, t_lo], axis=1), lora_ref[0], preferred_element_type=F32)
    kd_sum = jnp.zeros_like(k)
    for d in range(2):
        y = w0[d:d + 1, :] + low_rank[:, d * BR_W:(d + 1) * BR_W]
        a = _sigmoid(a0[d:d + 1, :] + low_rank[:, (2 + d) * BR_W:(3 + d) * BR_W])
        kd = k * (1.0 + (a - 1.0) * ka)
        c0 = (PREP_DIR + 3 * d) * BR_W
        o_ref[0, rows, c0:c0 + BR_W] = _sigmoid(y) * (-DECAY_SCALE)
        o_ref[0, rows, c0 + BR_W:c0 + 2 * BR_W] = kd
        o_ref[0, rows, c0 + 2 * BR_W:c0 + 3 * BR_W] = kk * a
        kd_sum = kd_sum + kd
    o_ref[0, rows, PREP_BONUS * BR_W:] = _mm_exact_rhs(r * kd_sum * rk, ones) * v


OFF_GM = 0
OFF_NA = OFF_GM + SEG_GM
OFF_RWIN = OFF_NA + SEG_NA
OFF_RWG = OFF_RWIN + SEG_RWIN
OFF_PL = OFF_RWG + SEG_RWG


def _projmix_kernel(x_ref, xp_ref, xn_ref, ng_ref, sh_ref, sc_ref, w_ref,
                    lng_ref, lnb_ref, ws_ref, bs_ref,
                    mu_ref, kkw_ref, ka_ref, rk_ref, w0_ref, a0_ref, lora_ref,
                    gm_ref, na_ref, prep_ref, rwg_ref, pl_ref):
    j = pl.program_id(1)
    tm = x_ref.shape[1]
    g, sh, sc = ng_ref[0], sh_ref[0, 0], sc_ref[0, 0]
    h = _norm_mod(x_ref[0], g, sh, sc).astype(BF16)
    proj = lambda lhs, off, width: jnp.dot(lhs, w_ref[0, :, off:off + width], preferred_element_type=F32)

    xh = jnp.concatenate([xn_ref[0], xp_ref[0]], axis=0)
    z_rw = proj(jnp.concatenate([h, _norm_mod(xh, g, sh, sc).astype(BF16)], axis=0), OFF_RWIN, SEG_RWIN)
    row = lax.broadcasted_iota(jnp.int32, (8, SEG_RWIN), 0)
    has_prev = jnp.where(j > 0, 1.0, 0.0)
    has_next = jnp.where(j < pl.num_programs(1) - 1, 1.0, 0.0)
    halo = (jnp.where(row == 0, z_rw[tm:tm + 8], 0.0) * has_next
            + jnp.where(row == 7, z_rw[tm + 8:tm + 16], 0.0) * has_prev)
    zf = _token_shift(z_rw[0:tm], halo, mu_ref[0])

    z_gm = proj(h, OFF_GM, SEG_GM)
    _rwkv_prepare(zf, kkw_ref[0], ka_ref[0], rk_ref[0], w0_ref[0], a0_ref[0], lora_ref, prep_ref, slice(None))
    na_ref[0] = proj(h, OFF_NA, SEG_NA).astype(na_ref.dtype)
    rwg_ref[0] = proj(h, OFF_RWG, SEG_RWG).astype(rwg_ref.dtype)
    pl_ref[0] = proj(h, OFF_PL, SEG_PL).astype(pl_ref.dtype)
    _gmlp_mix(z_gm, lng_ref[0], lnb_ref[0], ws_ref, bs_ref[0], gm_ref, 0)


def _project_mix(x, layer, mod, mod_row, p):
    b, t, d = x.shape
    tm = min(t, 256)
    nblk8 = t // 8
    row = lambda i, j: (i, j, 0)
    modspec = lambda c: pl.BlockSpec((1, 1, 1, d), lambda i, j: (layer, mod_row(i), 0, c))
    widths = (BR_W, SEG_NA, PREP_COLS, SEG_RWG, SEG_PL)
    dtypes = (BF16, BF16, F32, BF16, BF16)
    return pl.pallas_call(
        _projmix_kernel,
        grid=(b, t // tm),
        in_specs=[pl.BlockSpec((1, tm, d), row),
                  pl.BlockSpec((1, 8, d), lambda i, j: (i, jnp.maximum(j * (tm // 8) - 1, 0), 0)),
                  pl.BlockSpec((1, 8, d), lambda i, j: (i, jnp.minimum((j + 1) * (tm // 8), nblk8 - 1), 0)),
                  _layer_spec((1, d), layer), modspec(0), modspec(1),
                  _layer_spec((d, MIX_COLS), layer),
                  _layer_spec((1, BR_W), layer), _layer_spec((1, BR_W), layer),
                  _layer_spec((N_HEADS, CHUNK, CHUNK), layer), _layer_spec((CHUNK, BR_W), layer),
                  _layer_spec((1, SEG_RWIN), layer), _layer_spec((1, BR_W), layer),
                  _layer_spec((1, BR_W), layer), _layer_spec((1, BR_W), layer),
                  _layer_spec((2, BR_W), layer), _layer_spec((2, BR_W), layer),
                  _layer_spec((4 * LORA, 4 * BR_W), layer)],
        out_specs=[pl.BlockSpec((1, tm, w), row) for w in widths],
        out_shape=[jax.ShapeDtypeStruct((b, t, w), dt) for w, dt in zip(widths, dtypes)],
        compiler_params=_params("parallel", "parallel"),
        name="norm_in_proj_mix",
    )(x, x, x, p['norm_g'], mod, mod, p['w_mix'],
      p['gm_ln_g'], p['gm_ln_b'], p['gm_ws'], p['gm_bs'],
      p['rw_mu'], p['rw_kk'], p['rw_ka'], p['rw_rk'], p['rw_w0'], p['rw_a0'], p['rw_lora'])


def _pool_kernel(p_ref, g_ref, w_ref, sc_ref, o_ref, ext_ref, *, seq_len):
    j = pl.program_id(1)
    tm = o_ref.shape[1]
    n = tm + 2 * POOL_HALO

    @pl.when(j == 0)
    def _():
        ext_ref[0:POOL_HALO, :] = jnp.zeros((POOL_HALO, BR_W), F32)
        ext_ref[POOL_HALO + seq_len:, :] = jnp.zeros((POOL_HALO, BR_W), F32)
        ext_ref[POOL_HALO:POOL_HALO + seq_len, :] = p_ref[0].astype(F32)

    t0 = pl.multiple_of(j * tm, 8)
    x = ext_ref[pl.ds(t0, n), :]
    s2 = x + pltpu.roll(x, 1, 0)
    s4 = pltpu.roll(s2, 1, 0) + pltpu.roll(s2, n - 1, 0)
    s8 = pltpu.roll(s4, 2, 0) + pltpu.roll(s4, n - 2, 0)
    s16 = pltpu.roll(s8, 4, 0) + pltpu.roll(s8, n - 4, 0)
    mid = slice(POOL_HALO, POOL_HALO + tm)
    grp = _head_of_lane((tm, BR_W))
    ssum = jnp.where(grp == 0, s2[mid], jnp.where(grp == 1, s4[mid], jnp.where(grp == 2, s8[mid], s16[mid])))
    half = jnp.where(grp == 0, 1, jnp.where(grp == 1, 2, jnp.where(grp == 2, 4, 8)))
    t = t0 + lax.broadcasted_iota(jnp.int32, (tm, BR_W), 0)
    cnt = jnp.minimum(t + half, seq_len) - jnp.maximum(t - half, 0)
    d = ssum / cnt.astype(F32) - x[mid]
    y = _mm(d, w_ref[0])
    o_ref[0] = (y * sc_ref[0] * _silu(g_ref[0].astype(F32))).astype(o_ref.dtype)


def _pool(plz, layer, p):
    b, t, _ = plz.shape
    tm = min(t, 512)
    return pl.pallas_call(
        functools.partial(_pool_kernel, seq_len=t),
        grid=(b, t // tm),
        in_specs=[pl.BlockSpec((1, t, BR_W), lambda i, j: (i, 0, 0)),
                  pl.BlockSpec((1, tm, BR_W), lambda i, j: (i, j, 1)),
                  _layer_spec((BR_W, BR_W), layer),
                  _layer_spec((1, BR_W), layer)],
        out_specs=pl.BlockSpec((1, tm, BR_W), lambda i, j: (i, j, 0)),
        out_shape=jax.ShapeDtypeStruct((b, t, BR_W), BF16),
        scratch_shapes=[pltpu.VMEM((t + 2 * POOL_HALO, BR_W), F32)],
        compiler_params=_params("parallel", "arbitrary"),
        name="pool_mixer",
    )(plz, plz, p['pl_w'], p['pl_scale'])


def _stack_heads(q):
    head = _head_of_lane(q.shape)
    return jnp.concatenate([jnp.where(head == h, q, 0.0) for h in range(N_HEADS)], axis=0)


def _unstack_heads(o, n):
    head = _head_of_lane((n, BR_W))
    out = jnp.zeros((n, BR_W), F32)
    for h in range(N_HEADS):
        out = jnp.where(head == h, o[h * n:(h + 1) * n], out)
    return out


def _natten_kernel(q_ref, k_ref, v_ref, g_ref, kc_ref, vc_ref, bias_ref, o_ref, *, rows):
    band = NA_ROWS * GRID_W
    rng = range(NA_RT)
    kc = kc_ref[0]
    vc = vc_ref[0]
    r = [pl.program_id(1) * NA_RT + i for i in rng]
    rs = [jnp.clip(r[i] - NA_ROWS // 2, 0, rows - NA_ROWS) for i in rng]
    start = [pl.multiple_of(rs[i] * GRID_W, GRID_W) for i in rng]
    qs = [_stack_heads(q_ref[0, i * GRID_W:(i + 1) * GRID_W, :] * (HEAD_DIM ** -0.5)) for i in rng]
    kb = [k_ref[0, pl.ds(start[i], band), :] for i in rng]
    vb = [v_ref[0, pl.ds(start[i], band), :] for i in rng]
    s_b = [_mm_nt(qs[i], kb[i]) + bias_ref[0, rs[i] - r[i] + NA_ROWS - 1] for i in rng]
    s_c = [_mm_nt(qs[i], kc) for i in rng]
    m = [jnp.maximum(jnp.max(s_b[i], axis=1, keepdims=True), jnp.max(s_c[i], axis=1, keepdims=True)) for i in rng]
    e_b = [jnp.exp(s_b[i] - m[i]) for i in rng]
    e_c = [jnp.exp(s_c[i] - m[i]) for i in rng]
    den = [jnp.sum(e_b[i], axis=1, keepdims=True) + jnp.sum(e_c[i], axis=1, keepdims=True) for i in rng]
    pv = [(_mm(e_b[i], vb[i]) + _mm(e_c[i], vc)) / den[i] for i in rng]
    for i in rng:
        rows_i = slice(i * GRID_W, (i + 1) * GRID_W)
        gate = _silu(g_ref[0, rows_i, :].astype(F32))
        o_ref[0, rows_i, :] = (_unstack_heads(pv[i], GRID_W) * gate).astype(o_ref.dtype)


def _natten_bias(rpb):
    depth = rpb.shape[0]
    c = np.arange(GRID_W)
    c_start = np.clip(c - NA_COLS // 2, 0, GRID_W - NA_COLS)
    col_ok = (c[None, :] >= c_start[:, None]) & (c[None, :] < c_start[:, None] + NA_COLS)
    dc = np.clip(c[None, :] - c[:, None], -(NA_COLS - 1), NA_COLS - 1) + NA_COLS - 1
    select = (np.arange(2 * NA_COLS - 1)[:, None, None] == dc[None]).astype(np.float32)
    rows = jnp.stack([rpb[:, :, o:o + NA_ROWS] for o in range(NA_ROWS)], axis=1)
    tab = jnp.einsum('lohic,cqk->lohqik', rows, jnp.asarray(select), precision=lax.Precision.HIGHEST)
    tab = jnp.where(jnp.asarray(col_ok)[None, None, None, :, None, :], tab, NEG)
    return tab.reshape(depth, NA_ROWS, N_HEADS * GRID_W, NA_ROWS * GRID_W)


def _natten(na, na_ctx, layer, bias):
    b, t, _ = na.shape
    lc = na_ctx.shape[1]
    rows = t // GRID_W
    band = NA_ROWS * GRID_W
    tq = NA_RT * GRID_W
    qrow = lambda c: (lambda i, r: (i, r, c))
    full = lambda c: (lambda i, r: (i, 0, c))
    return pl.pallas_call(
        functools.partial(_natten_kernel, rows=rows),
        grid=(b, rows // NA_RT),
        in_specs=[pl.BlockSpec((1, tq, BR_W), qrow(0)),
                  pl.BlockSpec((1, t, BR_W), full(1)),
                  pl.BlockSpec((1, t, BR_W), full(2)),
                  pl.BlockSpec((1, tq, BR_W), qrow(3)),
                  pl.BlockSpec((1, lc, BR_W), full(1)),
                  pl.BlockSpec((1, lc, BR_W), full(2)),
                  _layer_spec((NA_ROWS, N_HEADS * GRID_W, band), layer)],
        out_specs=pl.BlockSpec((1, tq, BR_W), qrow(0)),
        out_shape=jax.ShapeDtypeStruct((b, t, BR_W), BF16),
        compiler_params=_params("parallel", "arbitrary"),
        name="natten_mixer",
    )(na, na, na, na, na_ctx, na_ctx, bias)


def _ctx_attn_kernel(q_ref, k_ref, v_ref, g_ref, o_ref):
    n = q_ref.shape[1]
    qs = _stack_heads(q_ref[0] * (HEAD_DIM ** -0.5))
    s = _mm_nt(qs, k_ref[0])
    e = jnp.exp(s - jnp.max(s, axis=1, keepdims=True))
    pv = _mm(e, v_ref[0]) / jnp.sum(e, axis=1, keepdims=True)
    o_ref[0] = (_unstack_heads(pv, n) * _silu(g_ref[0].astype(F32))).astype(o_ref.dtype)


def _ctx_attention(na_ctx):
    b, lc, _ = na_ctx.shape
    tq = 64
    qrow = lambda c: (lambda i, j: (i, j, c))
    full = lambda c: (lambda i, j: (i, 0, c))
    return pl.pallas_call(
        _ctx_attn_kernel,
        grid=(b, lc // tq),
        in_specs=[pl.BlockSpec((1, tq, BR_W), qrow(0)),
                  pl.BlockSpec((1, lc, BR_W), full(1)),
                  pl.BlockSpec((1, lc, BR_W), full(2)),
                  pl.BlockSpec((1, tq, BR_W), qrow(3))],
        out_specs=pl.BlockSpec((1, tq, BR_W), qrow(0)),
        out_shape=jax.ShapeDtypeStruct((b, lc, BR_W), BF16),
        compiler_params=_params("parallel", "arbitrary"),
        name="ctx_attention",
    )(na_ctx, na_ctx, na_ctx, na_ctx)


def _stack_pair(x):
    head = _head_of_lane(x.shape)
    return jnp.concatenate([jnp.where(head == 0, x, 0.0), jnp.where(head == 1, x, 0.0)], axis=0)


def _scan_chunks(chains):
    n = chains[0][0].shape[0]
    m = 2 * n
    nc = len(chains)
    rng = range(nc)
    rev = [c[7] for c in chains]
    ti = lax.broadcasted_iota(jnp.int32, (n, n), 0)
    si = lax.broadcasted_iota(jnp.int32, (n, n), 1)
    tri = {False: jnp.where(si <= ti, 1.0, 0.0).astype(BF16), True: jnp.where(si >= ti, 1.0, 0.0).astype(BF16)}
    tri3 = {k: jnp.concatenate([t, t, t], axis=1) for k, t in tri.items()}
    i2 = lax.broadcasted_iota(jnp.int32, (m, 2 * m), 0)
    j2 = lax.broadcasted_iota(jnp.int32, (m, 2 * m), 1) % n
    t2 = i2 % n
    same = jnp.where(i2 < n, 0, 1)
    keep = {False: j2 < t2 + same, True: j2 > t2 - same}
    ri = lax.broadcasted_iota(jnp.int32, (LANES, LANES), 0)
    ci = lax.broadcasted_iota(jnp.int32, (LANES, LANES), 1)
    eye = ri == ci
    same_head = (ri // HEAD_DIM) == (ci // HEAD_DIM)
    d = functools.partial(jnp.dot, preferred_element_type=F32)

    def pieces3(x):
        x0 = x.astype(BF16)
        r1 = x - x0.astype(F32)
        x1 = r1.astype(BF16)
        return jnp.concatenate([x0, x1, (r1 - x1.astype(F32)).astype(BF16)], axis=0)

    cum = [d(tri3[rev[c]], pieces3(chains[c][3])) for c in rng]
    tot = [jnp.sum(chains[c][3], axis=0, keepdims=True) for c in rng]
    e_neg = [jnp.exp(-cum[c]) for c in rng]
    ar = [jnp.concatenate([chains[c][2] * jnp.exp(cum[c] - chains[c][3]),
                           chains[c][0] * jnp.exp(cum[c])], axis=0).astype(BF16) for c in rng]
    bk_st = [jnp.concatenate([_stack_pair((chains[c][5] * e_neg[c]).astype(BF16)),
                              _stack_pair((chains[c][4] * e_neg[c]).astype(BF16))], axis=0) for c in rng]
    v_bf = [chains[c][1].astype(BF16) for c in rng]

    g = [jnp.where(keep[rev[c]], _mm_nt(ar[c], bk_st[c]), 0.0) for c in rng]
    lhs = [jnp.concatenate([ar[c], g[c][:, m:].astype(BF16)], axis=1) for c in rng]
    rhs = [jnp.concatenate([chains[c][6].astype(BF16), _stack_pair(v_bf[c])], axis=0) for c in rng]
    xy = [d(lhs[c], rhs[c]) for c in rng]

    u = [xy[c][:n] for c in rng]
    p = [g[c][:n, :m] for c in rng]
    step = 1
    while step < n:
        step *= 2
        last = step >= n
        nxt_u, nxt_p = [], []
        for c in rng:
            p_bf = p[c].astype(BF16)
            u_hi, u_lo = _split(u[c])
            rhs_parts = [_stack_pair(u_hi), _stack_pair(u_lo)] + ([] if last else [_stack_pair(p_bf)])
            acc = d(p_bf, jnp.concatenate(rhs_parts, axis=1))
            nxt_u.append(u[c] + (acc[:, :LANES] + acc[:, LANES:2 * LANES]))
            nxt_p.append(None if last else acc[:, 2 * LANES:])
        u, p = nxt_u, nxt_p

    outs = []
    for c in rng:
        u_bf = u[c].astype(BF16)
        y = xy[c][n:] + d(g[c][n:, :m].astype(BF16), _stack_pair(u_bf))
        e_rest = jnp.exp(tot[c] - cum[c])
        bk_rest = jnp.concatenate([chains[c][5] * e_rest, chains[c][4] * e_rest], axis=0)
        decay = jnp.sum(jnp.where(eye, jnp.exp(tot[c]), 0.0), axis=1, keepdims=True)
        upd = _mm_tn(bk_rest, jnp.concatenate([u_bf, v_bf[c]], axis=0))
        outs.append((y, chains[c][6] * decay + jnp.where(same_head, upd, 0.0)))
    return outs


def _scan_kernel(sf_ref, df_ref, sb_ref, db_ref, s0_ref, yf_ref, yb_ref, sfin_ref, state_ref):
    j = pl.program_id(1)

    @pl.when(j == 0)
    def _():
        state_ref[...] = s0_ref[...]

    nb = sf_ref.shape[0]
    npair = BR_W // LANES
    chains = []
    for i in range(nb):
        for dr, (s_ref, d_ref) in enumerate(((sf_ref, df_ref), (sb_ref, db_ref))):
            for p in range(npair):
                col = lambda blk: slice(blk * BR_W + p * LANES, blk * BR_W + (p + 1) * LANES)
                chains.append((s_ref[i, :, col(PREP_R)], s_ref[i, :, col(PREP_V)], s_ref[i, :, col(PREP_A)],
                               d_ref[i, :, col(0)], d_ref[i, :, col(1)], d_ref[i, :, col(2)],
                               state_ref[i, dr, p], dr == 1))
    outs = iter(_scan_chunks(chains))
    for i in range(nb):
        for dr, y_ref in enumerate((yf_ref, yb_ref)):
            for p in range(npair):
                y, new_state = next(outs)
                y_ref[i, :, p * LANES:(p + 1) * LANES] = y
                state_ref[i, dr, p] = new_state

    @pl.when(j == pl.num_programs(1) - 1)
    def _():
        sfin_ref[...] = state_ref[...]


def _rwkv_scan(prep, s0):
    b, t, _ = prep.shape
    n = t // SCAN_L
    npair = BR_W // LANES
    blk3 = 3 * BR_W
    fwd = lambda c: (lambda i, j: (i, j, c))
    bwd = lambda c: (lambda i, j: (i, n - 1 - j, c))
    st = lambda i, j: (i, 0, 0, 0, 0)
    nb = SCAN_NB if b % SCAN_NB == 0 else 1
    return pl.pallas_call(
        _scan_kernel,
        grid=(b // nb, n),
        in_specs=[pl.BlockSpec((nb, SCAN_L, blk3), fwd(0)),
                  pl.BlockSpec((nb, SCAN_L, blk3), fwd(1)),
                  pl.BlockSpec((nb, SCAN_L, blk3), bwd(0)),
                  pl.BlockSpec((nb, SCAN_L, blk3), bwd(2)),
                  pl.BlockSpec((nb, 2, npair, LANES, LANES), st)],
        out_specs=[pl.BlockSpec((nb, SCAN_L, BR_W), fwd(0)),
                   pl.BlockSpec((nb, SCAN_L, BR_W), bwd(0)),
                   pl.BlockSpec((nb, 2, npair, LANES, LANES), st)],
        out_shape=[jax.ShapeDtypeStruct((b, t, BR_W), F32),
                   jax.ShapeDtypeStruct((b, t, BR_W), F32),
                   jax.ShapeDtypeStruct((b, 2, npair, LANES, LANES), F32)],
        scratch_shapes=[pltpu.VMEM((nb, 2, npair, LANES, LANES), F32)],
        compiler_params=_params("parallel", "arbitrary"),
        name="rwkv_scan",
    )(prep, prep, prep, prep, s0)


def _rwkv_finish(y_sum, bonus, gate, gn_g, gn_b):
    avg = _group_matrix(1.0 / HEAD_DIM)
    mu = _mm_exact_rhs(y_sum, avg, precise=True)
    oc = y_sum - mu
    var = _mm_exact_rhs(oc * oc, avg)
    o = oc * lax.rsqrt(var + GN_EPS) * gn_g + gn_b + bonus
    return o * _silu(gate)


def _merge_kernel(x_ref, g_ref, sh_ref, sc_ref, gt_ref, gm_ref, na_ref, yf_ref, yb_ref, bonus_ref, rwg_ref,
                  gng_ref, gnb_ref, pl_ref, wm_ref, wbr_ref, wout_ref, fin_ref, o_ref, *, final):
    x = x_ref[0]
    d = x.shape[-1]
    h = _norm_mod(x, g_ref[0], sh_ref[0, 0], sc_ref[0, 0]).astype(BF16)
    rw = _rwkv_finish(yf_ref[0] + yb_ref[0], bonus_ref[0], rwg_ref[0].astype(F32), gng_ref[0], gnb_ref[0])
    y = None
    for i, br in enumerate((gm_ref[0], na_ref[0], rw.astype(BF16), pl_ref[0])):
        logits = jnp.dot(h, wm_ref[0, :, i * d:(i + 1) * d], preferred_element_type=F32)
        term = _sigmoid_t(logits) * jnp.dot(br, wbr_ref[0, i], preferred_element_type=F32)
        y = term if y is None else y + term
    out = x + gt_ref[0, 0] * jnp.dot(y.astype(BF16), wout_ref[0], preferred_element_type=F32)
    if final:
        ms = jnp.mean(out * out, axis=-1, keepdims=True)
        out = out * lax.rsqrt(ms + EPS) * fin_ref[...]
    o_ref[0] = out


def _merge(x, layer, mod, mod_row, gm, na, yf, yb, prep, rwg, plb, p, final):
    b, t, d = x.shape
    tm = min(t, 512)
    row = lambda i, j: (i, j, 0)
    modspec = lambda c: pl.BlockSpec((1, 1, 1, d), lambda i, j: (layer, mod_row(i), 0, c))
    br = pl.BlockSpec((1, tm, BR_W), row)
    return pl.pallas_call(
        functools.partial(_merge_kernel, final=final),
        grid=(b, t // tm),
        in_specs=[pl.BlockSpec((1, tm, d), row), _layer_spec((1, d), layer), modspec(0), modspec(1), modspec(2),
                  br, br, br, br, pl.BlockSpec((1, tm, BR_W), lambda i, j: (i, j, PREP_BONUS)), br,
                  _layer_spec((1, BR_W), layer), _layer_spec((1, BR_W), layer), br,
                  _layer_spec((d, N_BRANCH * d), layer), _layer_spec((N_BRANCH, BR_W, d), layer),
                  _layer_spec((d, d), layer), pl.BlockSpec((1, d), lambda i, j: (0, 0))],
        out_specs=pl.BlockSpec((1, tm, d), row),
        out_shape=jax.ShapeDtypeStruct((b, t, d), F32),
        compiler_params=_params("parallel", "parallel"),
        name="merge_out_proj",
    )(x, p['norm_g'], mod, mod, mod, gm, na, yf, yb, prep, rwg, p['rw_gn_g'], p['rw_gn_b'], plb,
      p['w_merge'], p['w_br'], p['w_out'], p['final_g'])


def _layer(x, ctx, layer, mod, p, last):
    b = x.shape[0]
    latent_row = lambda i: i
    ctx_row = lambda i: b

    gm_c, na_c, prep_c, rwg_c, pl_c = _project_mix(ctx, layer, mod, ctx_row, p)
    gm_x, na_x, prep_x, rwg_x, pl_x = _project_mix(x, layer, mod, latent_row, p)

    zero_state = jnp.zeros((b, 2, BR_W // LANES, LANES, LANES), F32)
    yf_c, yb_c, s_ctx = _rwkv_scan(prep_c, zero_state)
    yf_x, yb_x, _ = _rwkv_scan(prep_x, s_ctx)

    x_new = _merge(x, layer, mod, latent_row, gm_x, _natten(na_x, na_c, layer, p['na_bias']),
                   yf_x, yb_x, prep_x, rwg_x, _pool(pl_x, layer, p), p, final=last)
    if last:
        return x_new, ctx
    ctx_new = _merge(ctx, layer, mod, ctx_row, gm_c, _ctx_attention(na_c),
                     yf_c, yb_c, prep_c, rwg_c, _pool(pl_c, layer, p), p, final=False)
    return x_new, ctx_new


def kernel(x, c, ctx, c_ctx, ada_w, ada_b, norm_g, w_in, gm_ln_g, gm_ln_b, gm_ws, gm_bs, na_rpb, rw_mu, rw_w0, rw_w2, rw_a0, rw_a2, rw_kk, rw_ka, rw_rk, rw_gn_g, rw_gn_b, pl_w, pl_scale, w_br, w_out, final_g):
    b, _, d = x.shape
    depth = ada_w.shape[0]
    mod_rows = 16
    cs = jnp.zeros((mod_rows, d), F32).at[:b].set(c).at[b].set(c_ctx)
    mod = _modulation(cs, ada_w, ada_b).reshape(depth, mod_rows, 1, 3 * d)

    vec = lambda a: a.reshape(depth, 1, -1)
    w_bf = w_in.astype(BF16)
    lora_zeros = jnp.zeros((depth, LORA, 2 * BR_W), F32)
    side = lambda w: jnp.concatenate([w[:, 0], w[:, 1]], axis=2)
    lora = jnp.concatenate([jnp.concatenate([side(rw_w2), lora_zeros], axis=2),
                            jnp.concatenate([lora_zeros, side(rw_a2)], axis=2)], axis=1).astype(BF16)
    p = {
        'norm_g': vec(norm_g), 'final_g': final_g.reshape(1, d),
        'w_mix': w_bf[:, :, :MIX_COLS], 'w_merge': w_bf[:, :, MIX_COLS:],
        'w_br': w_br.astype(BF16), 'w_out': w_out.astype(BF16),
        'gm_ln_g': vec(gm_ln_g), 'gm_ln_b': vec(gm_ln_b), 'gm_ws': gm_ws.astype(BF16),
        'gm_bs': jnp.repeat(gm_bs.transpose(0, 2, 1), HEAD_DIM, axis=2),
        'na_bias': _natten_bias(na_rpb),
        'rw_mu': vec(rw_mu), 'rw_kk': vec(rw_kk), 'rw_ka': vec(rw_ka), 'rw_rk': vec(rw_rk),
        'rw_w0': rw_w0, 'rw_a0': rw_a0,
        'rw_lora': jnp.concatenate([lora, lora], axis=1),
        'rw_gn_g': vec(rw_gn_g), 'rw_gn_b': vec(rw_gn_b),
        'pl_w': jnp.einsum('lgcd,gh->lgchd', pl_w, jnp.eye(N_HEADS, dtype=F32)).reshape(depth, BR_W, BR_W).astype(BF16),
        'pl_scale': vec(pl_scale),
    }
    for layer in range(depth):
        x, ctx = _layer(x, ctx, layer, mod, p, last=(layer == depth - 1))
    return x
```

```python
import functools

import jax
import jax.numpy as jnp
import numpy as np
from jax import lax
from jax.experimental import pallas as pl
from jax.experimental.pallas import tpu as pltpu

F32 = jnp.float32
BF16 = jnp.bfloat16

N_HEADS = 4
N_BRANCH = 4
HEAD_DIM = 64
BR_W = N_HEADS * HEAD_DIM
LANES = 128
GRID_W = 64
NA_ROWS = 8
NA_COLS = 16
NA_RT = 8
CHUNK = 128
LORA = 64
POOL_WINDOWS = (2, 4, 8, 16)
POOL_HALO = 16
SCAN_L = 64
SCAN_PAIR_STEPS = 16
SCAN_NB = 4
EPS = 1e-6
GN_EPS = 64e-5
NEG = -1e30
VMEM_LIMIT = 56 * 1024 * 1024

SEG_GM, SEG_NA, SEG_RWIN, SEG_RWG, SEG_PL = 3 * BR_W, 4 * BR_W, 3 * BR_W + 2 * LORA, BR_W, 2 * BR_W
MIX_COLS = SEG_GM + SEG_NA + SEG_RWIN + SEG_RWG + SEG_PL


def _params(*sem):
    return pltpu.CompilerParams(dimension_semantics=sem, vmem_limit_bytes=VMEM_LIMIT)


def _sigmoid(x):
    return 1.0 / (1.0 + jnp.exp(-x))


def _sigmoid_t(x):
    return 0.5 * jnp.tanh(0.5 * x) + 0.5


def _silu(x):
    return x * _sigmoid_t(x)


def _gelu(x):
    return 0.5 * x * (1.0 + jnp.tanh(0.7978845608028654 * (x + 0.044715 * x * x * x)))


def _mm(a, b):
    return jnp.dot(a.astype(BF16), b.astype(BF16), preferred_element_type=F32)


def _mm_nt(a, b):
    return lax.dot_general(a.astype(BF16), b.astype(BF16), (((1,), (1,)), ((), ())),
                           preferred_element_type=F32)


def _mm_tn(a, b):
    return lax.dot_general(a.astype(BF16), b.astype(BF16), (((0,), (0,)), ((), ())),
                           preferred_element_type=F32)


def _split(x):
    hi = x.astype(BF16)
    lo = (x - hi.astype(F32)).astype(BF16)
    return hi, lo


def _mm3(a, b):
    ah, al = _split(a)
    bh, bl = _split(b)
    d = functools.partial(jnp.dot, preferred_element_type=F32)
    return d(ah, bh) + d(ah, bl) + d(al, bh)


def _mm_exact_rhs(a, b_bf16, precise=False):
    d = functools.partial(jnp.dot, preferred_element_type=F32)
    if not precise:
        return d(a.astype(BF16), b_bf16)
    a_hi, a_lo = _split(a)
    return d(a_hi, b_bf16) + d(a_lo, b_bf16)


def _group_matrix(value):
    i = lax.broadcasted_iota(jnp.int32, (BR_W, BR_W), 0) // HEAD_DIM
    j = lax.broadcasted_iota(jnp.int32, (BR_W, BR_W), 1) // HEAD_DIM
    return jnp.where(i == j, value, 0.0).astype(BF16)


def _head_of_lane(shape):
    return lax.broadcasted_iota(jnp.int32, shape, len(shape) - 1) // HEAD_DIM


def _norm_mod(x, g, shift, scale):
    ms = jnp.mean(x * x, axis=-1, keepdims=True)
    return (x * lax.rsqrt(ms + EPS) * g) * (1.0 + scale) + shift


def _mod_kernel(cs_ref, w_ref, b_ref, o_ref):
    o_ref[0] = _mm3(_silu(cs_ref[...]), w_ref[0]) + b_ref[0]


def _modulation(cs, ada_w, ada_b):
    depth, d, d3 = ada_w.shape
    rows = cs.shape[0]
    tn = 768
    return pl.pallas_call(
        _mod_kernel,
        grid=(depth, d3 // tn),
        in_specs=[pl.BlockSpec((rows, d), lambda l, j: (0, 0)),
                  pl.BlockSpec((1, d, tn), lambda l, j: (l, 0, j)),
                  pl.BlockSpec((1, 1, tn), lambda l, j: (l, 0, j))],
        out_specs=pl.BlockSpec((1, rows, tn), lambda l, j: (l, 0, j)),
        out_shape=jax.ShapeDtypeStruct((depth, rows, d3), F32),
        compiler_params=_params("arbitrary", "arbitrary"),
        name="adaln_modulation",
    )(cs, ada_w, ada_b.reshape(depth, 1, d3))


def _layer_spec(tail, layer):
    zeros = (0,) * len(tail)
    return pl.BlockSpec((1,) + tuple(tail), lambda i, j: (layer,) + zeros)


def _gmlp_mix(z, lng, lnb, ws_ref, bs, o_ref, row0):
    tm = z.shape[0]
    u = _gelu(z[:, 0:BR_W])
    v = _gelu(z[:, BR_W:2 * BR_W])
    gate = z[:, 2 * BR_W:3 * BR_W]
    avg = _group_matrix(1.0 / HEAD_DIM)
    mu = _mm_exact_rhs(v, avg, precise=True)
    vc = v - mu
    var = _mm_exact_rhs(vc * vc, avg)
    vh = (vc * lax.rsqrt(var + EPS) * lng + lnb).astype(BF16)
    head = _head_of_lane((CHUNK, BR_W))
    su = u * _silu(gate)
    for c in range(tm // CHUNK):
        rows = slice(c * CHUNK, (c + 1) * CHUNK)
        vch = vh[rows]
        mixed = jnp.zeros((CHUNK, BR_W), F32)
        for g in range(N_HEADS):
            mg = jnp.dot(ws_ref[0, g], vch, preferred_element_type=F32)
            mixed = jnp.where(head == g, mg, mixed)
        o_ref[0, row0 + c * CHUNK:row0 + (c + 1) * CHUNK, :] = (su[rows] * (mixed + bs)).astype(o_ref.dtype)


PREP_R, PREP_V, PREP_A = 0, 1, 2
PREP_DIR = 3
PREP_BONUS = 9
PREP_COLS = 10 * BR_W
DECAY_SCALE = 0.6065306597126334


def _token_shift(z, halo, mu):
    tm = z.shape[0]
    n = tm + 8
    ze = jnp.concatenate([z, halo], axis=0)
    neighbours = pltpu.roll(ze, 1, 0)[0:tm] + pltpu.roll(ze, n - 1, 0)[0:tm]
    return z * (1.0 - mu) + neighbours * (0.5 * mu)


def _rwkv_prepare(zf, kkw, ka, rk, w0, a0, lora_ref, o_ref, rows):
    r = zf[:, 0:BR_W]
    k = zf[:, BR_W:2 * BR_W]
    v = zf[:, 2 * BR_W:3 * BR_W]
    lo = zf[:, 3 * BR_W:3 * BR_W + 2 * LORA]
    ones = _group_matrix(1.0)
    kk = k * kkw
    kk = kk / jnp.maximum(jnp.sqrt(_mm_exact_rhs(kk * kk, ones)), 1e-12)
    o_ref[0, rows, PREP_R * BR_W:(PREP_R + 1) * BR_W] = r
    o_ref[0, rows, PREP_V * BR_W:(PREP_V + 1) * BR_W] = v
    o_ref[0, rows, PREP_A * BR_W:(PREP_A + 1) * BR_W] = -kk
    decay_lane = lax.broadcasted_iota(jnp.int32, lo.shape, 1) < LORA
    t_hi, t_lo = _split(jnp.where(decay_lane, jnp.tanh(lo), lo))
    low_rank = jnp.dot(jnp.concatenate([t_hi, t_lo], axis=1), lora_ref[0], preferred_element_type=F32)
    kd_sum = jnp.zeros_like(k)
    for d in range(2):
        y = w0[d:d + 1, :] + low_rank[:, d * BR_W:(d + 1) * BR_W]
        a = _sigmoid(a0[d:d + 1, :] + low_rank[:, (2 + d) * BR_W:(3 + d) * BR_W])
        kd = k * (1.0 + (a - 1.0) * ka)
        c0 = (PREP_DIR + 3 * d) * BR_W
        o_ref[0, rows, c0:c0 + BR_W] = _sigmoid(y) * (-DECAY_SCALE)
        o_ref[0, rows, c0 + BR_W:c0 + 2 * BR_W] = kd
        o_ref[0, rows, c0 + 2 * BR_W:c0 + 3 * BR_W] = kk * a
        kd_sum = kd_sum + kd
    o_ref[0, rows, PREP_BONUS * BR_W:] = _mm_exact_rhs(r * kd_sum * rk, ones) * v


OFF_GM = 0
OFF_NA = OFF_GM + SEG_GM
OFF_RWIN = OFF_NA + SEG_NA
OFF_RWG = OFF_RWIN + SEG_RWIN
OFF_PL = OFF_RWG + SEG_RWG


def _projmix_kernel(x_ref, xp_ref, xn_ref, ng_ref, sh_ref, sc_ref, w_ref,
                    lng_ref, lnb_ref, ws_ref, bs_ref,
                    mu_ref, kkw_ref, ka_ref, rk_ref, w0_ref, a0_ref, lora_ref,
                    gm_ref, na_ref, prep_ref, rwg_ref, pl_ref):
    j = pl.program_id(1)
    tm = x_ref.shape[1]
    g, sh, sc = ng_ref[0], sh_ref[0, 0], sc_ref[0, 0]
    h = _norm_mod(x_ref[0], g, sh, sc).astype(BF16)
    proj = lambda lhs, off, width: jnp.dot(lhs, w_ref[0, :, off:off + width], preferred_element_type=F32)

    xh = jnp.concatenate([xn_ref[0], xp_ref[0]], axis=0)
    z_rw = proj(jnp.concatenate([h, _norm_mod(xh, g, sh, sc).astype(BF16)], axis=0), OFF_RWIN, SEG_RWIN)
    row = lax.broadcasted_iota(jnp.int32, (8, SEG_RWIN), 0)
    has_prev = jnp.where(j > 0, 1.0, 0.0)
    has_next = jnp.where(j < pl.num_programs(1) - 1, 1.0, 0.0)
    halo = (jnp.where(row == 0, z_rw[tm:tm + 8], 0.0) * has_next
            + jnp.where(row == 7, z_rw[tm + 8:tm + 16], 0.0) * has_prev)
    zf = _token_shift(z_rw[0:tm], halo, mu_ref[0])

    z_gm = proj(h, OFF_GM, SEG_GM)
    _rwkv_prepare(zf, kkw_ref[0], ka_ref[0], rk_ref[0], w0_ref[0], a0_ref[0], lora_ref, prep_ref, slice(None))
    na_ref[0] = proj(h, OFF_NA, SEG_NA).astype(na_ref.dtype)
    rwg_ref[0] = proj(h, OFF_RWG, SEG_RWG).astype(rwg_ref.dtype)
    pl_ref[0] = proj(h, OFF_PL, SEG_PL).astype(pl_ref.dtype)
    _gmlp_mix(z_gm, lng_ref[0], lnb_ref[0], ws_ref, bs_ref[0], gm_ref, 0)


def _project_mix(x, layer, mod, mod_row, p):
    b, t, d = x.shape
    tm = min(t, 512)
    nblk8 = t // 8
    row = lambda i, j: (i, j, 0)
    modspec = lambda c: pl.BlockSpec((1, 1, 1, d), lambda i, j: (layer, mod_row(i), 0, c))
    widths = (BR_W, SEG_NA, PREP_COLS, SEG_RWG, SEG_PL)
    dtypes = (BF16, BF16, F32, BF16, BF16)
    return pl.pallas_call(
        _projmix_kernel,
        grid=(b, t // tm),
        in_specs=[pl.BlockSpec((1, tm, d), row),
                  pl.BlockSpec((1, 8, d), lambda i, j: (i, jnp.maximum(j * (tm // 8) - 1, 0), 0)),
                  pl.BlockSpec((1, 8, d), lambda i, j: (i, jnp.minimum((j + 1) * (tm // 8), nblk8 - 1), 0)),
                  _layer_spec((1, d), layer), modspec(0), modspec(1),
                  _layer_spec((d, MIX_COLS), layer),
                  _layer_spec((1, BR_W), layer), _layer_spec((1, BR_W), layer),
                  _layer_spec((N_HEADS, CHUNK, CHUNK), layer), _layer_spec((CHUNK, BR_W), layer),
                  _layer_spec((1, SEG_RWIN), layer), _layer_spec((1, BR_W), layer),
                  _layer_spec((1, BR_W), layer), _layer_spec((1, BR_W), layer),
                  _layer_spec((2, BR_W), layer), _layer_spec((2, BR_W), layer),
                  _layer_spec((4 * LORA, 4 * BR_W), layer)],
        out_specs=[pl.BlockSpec((1, tm, w), row) for w in widths],
        out_shape=[jax.ShapeDtypeStruct((b, t, w), dt) for w, dt in zip(widths, dtypes)],
        compiler_params=_params("parallel", "parallel"),
        name="norm_in_proj_mix",
    )(x, x, x, p['norm_g'], mod, mod, p['w_mix'],
      p['gm_ln_g'], p['gm_ln_b'], p['gm_ws'], p['gm_bs'],
      p['rw_mu'], p['rw_kk'], p['rw_ka'], p['rw_rk'], p['rw_w0'], p['rw_a0'], p['rw_lora'])


def _pool_kernel(p_ref, g_ref, w_ref, sc_ref, o_ref, ext_ref, *, seq_len):
    j = pl.program_id(1)
    tm = o_ref.shape[1]
    n = tm + 2 * POOL_HALO

    @pl.when(j == 0)
    def _():
        ext_ref[0:POOL_HALO, :] = jnp.zeros((POOL_HALO, BR_W), F32)
        ext_ref[POOL_HALO + seq_len:, :] = jnp.zeros((POOL_HALO, BR_W), F32)
        ext_ref[POOL_HALO:POOL_HALO + seq_len, :] = p_ref[0].astype(F32)

    t0 = pl.multiple_of(j * tm, 8)
    x = ext_ref[pl.ds(t0, n), :]
    s2 = x + pltpu.roll(x, 1, 0)
    s4 = pltpu.roll(s2, 1, 0) + pltpu.roll(s2, n - 1, 0)
    s8 = pltpu.roll(s4, 2, 0) + pltpu.roll(s4, n - 2, 0)
    s16 = pltpu.roll(s8, 4, 0) + pltpu.roll(s8, n - 4, 0)
    mid = slice(POOL_HALO, POOL_HALO + tm)
    grp = _head_of_lane((tm, BR_W))
    ssum = jnp.where(grp == 0, s2[mid], jnp.where(grp == 1, s4[mid], jnp.where(grp == 2, s8[mid], s16[mid])))
    half = jnp.where(grp == 0, 1, jnp.where(grp == 1, 2, jnp.where(grp == 2, 4, 8)))
    t = t0 + lax.broadcasted_iota(jnp.int32, (tm, BR_W), 0)
    cnt = jnp.minimum(t + half, seq_len) - jnp.maximum(t - half, 0)
    d = ssum / cnt.astype(F32) - x[mid]
    y = _mm(d, w_ref[0])
    o_ref[0] = (y * sc_ref[0] * _silu(g_ref[0].astype(F32))).astype(o_ref.dtype)


def _pool(plz, layer, p):
    b, t, _ = plz.shape
    tm = min(t, 512)
    return pl.pallas_call(
        functools.partial(_pool_kernel, seq_len=t),
        grid=(b, t // tm),
        in_specs=[pl.BlockSpec((1, t, BR_W), lambda i, j: (i, 0, 0)),
                  pl.BlockSpec((1, tm, BR_W), lambda i, j: (i, j, 1)),
                  _layer_spec((BR_W, BR_W), layer),
                  _layer_spec((1, BR_W), layer)],
        out_specs=pl.BlockSpec((1, tm, BR_W), lambda i, j: (i, j, 0)),
        out_shape=jax.ShapeDtypeStruct((b, t, BR_W), BF16),
        scratch_shapes=[pltpu.VMEM((t + 2 * POOL_HALO, BR_W), F32)],
        compiler_params=_params("parallel", "arbitrary"),
        name="pool_mixer",
    )(plz, plz, p['pl_w'], p['pl_scale'])


def _stack_heads(q):
    head = _head_of_lane(q.shape)
    return jnp.concatenate([jnp.where(head == h, q, 0.0) for h in range(N_HEADS)], axis=0)


def _unstack_heads(o, n):
    head = _head_of_lane((n, BR_W))
    out = jnp.zeros((n, BR_W), F32)
    for h in range(N_HEADS):
        out = jnp.where(head == h, o[h * n:(h + 1) * n], out)
    return out


def _natten_kernel(q_ref, k_ref, v_ref, g_ref, kc_ref, vc_ref, bias_ref, o_ref, *, rows):
    band = NA_ROWS * GRID_W
    rng = range(NA_RT)
    kc = kc_ref[0]
    vc = vc_ref[0]
    r = [pl.program_id(1) * NA_RT + i for i in rng]
    rs = [jnp.clip(r[i] - NA_ROWS // 2, 0, rows - NA_ROWS) for i in rng]
    start = [pl.multiple_of(rs[i] * GRID_W, GRID_W) for i in rng]
    qs = [_stack_heads(q_ref[0, i * GRID_W:(i + 1) * GRID_W, :] * (HEAD_DIM ** -0.5)) for i in rng]
    kb = [k_ref[0, pl.ds(start[i], band), :] for i in rng]
    vb = [v_ref[0, pl.ds(start[i], band), :] for i in rng]
    s_b = [_mm_nt(qs[i], kb[i]) + bias_ref[0, rs[i] - r[i] + NA_ROWS - 1] for i in rng]
    s_c = [_mm_nt(qs[i], kc) for i in rng]
    m = [jnp.maximum(jnp.max(s_b[i], axis=1, keepdims=True), jnp.max(s_c[i], axis=1, keepdims=True)) for i in rng]
    e_b = [jnp.exp(s_b[i] - m[i]) for i in rng]
    e_c = [jnp.exp(s_c[i] - m[i]) for i in rng]
    den = [jnp.sum(e_b[i], axis=1, keepdims=True) + jnp.sum(e_c[i], axis=1, keepdims=True) for i in rng]
    pv = [(_mm(e_b[i], vb[i]) + _mm(e_c[i], vc)) / den[i] for i in rng]
    for i in rng:
        rows_i = slice(i * GRID_W, (i + 1) * GRID_W)
        gate = _silu(g_ref[0, rows_i, :].astype(F32))
        o_ref[0, rows_i, :] = (_unstack_heads(pv[i], GRID_W) * gate).astype(o_ref.dtype)


def _natten_bias(rpb):
    depth = rpb.shape[0]
    c = np.arange(GRID_W)
    c_start = np.clip(c - NA_COLS // 2, 0, GRID_W - NA_COLS)
    col_ok = (c[None, :] >= c_start[:, None]) & (c[None, :] < c_start[:, None] + NA_COLS)
    dc = np.clip(c[None, :] - c[:, None], -(NA_COLS - 1), NA_COLS - 1) + NA_COLS - 1
    select = (np.arange(2 * NA_COLS - 1)[:, None, None] == dc[None]).astype(np.float32)
    rows = jnp.stack([rpb[:, :, o:o + NA_ROWS] for o in range(NA_ROWS)], axis=1)
    tab = jnp.einsum('lohic,cqk->lohqik', rows, jnp.asarray(select), precision=lax.Precision.HIGHEST)
    tab = jnp.where(jnp.asarray(col_ok)[None, None, None, :, None, :], tab, NEG)
    return tab.reshape(depth, NA_ROWS, N_HEADS * GRID_W, NA_ROWS * GRID_W)


def _natten(na, na_ctx, layer, bias):
    b, t, _ = na.shape
    lc = na_ctx.shape[1]
    rows = t // GRID_W
    band = NA_ROWS * GRID_W
    tq = NA_RT * GRID_W
    qrow = lambda c: (lambda i, r: (i, r, c))
    full = lambda c: (lambda i, r: (i, 0, c))
    return pl.pallas_call(
        functools.partial(_natten_kernel, rows=rows),
        grid=(b, rows // NA_RT),
        in_specs=[pl.BlockSpec((1, tq, BR_W), qrow(0)),
                  pl.BlockSpec((1, t, BR_W), full(1)),
                  pl.BlockSpec((1, t, BR_W), full(2)),
                  pl.BlockSpec((1, tq, BR_W), qrow(3)),
                  pl.BlockSpec((1, lc, BR_W), full(1)),
                  pl.BlockSpec((1, lc, BR_W), full(2)),
                  _layer_spec((NA_ROWS, N_HEADS * GRID_W, band), layer)],
        out_specs=pl.BlockSpec((1, tq, BR_W), qrow(0)),
        out_shape=jax.ShapeDtypeStruct((b, t, BR_W), BF16),
        compiler_params=_params("parallel", "arbitrary"),
        name="natten_mixer",
    )(na, na, na, na, na_ctx, na_ctx, bias)


def _ctx_attn_kernel(q_ref, k_ref, v_ref, g_ref, o_ref):
    n = q_ref.shape[1]
    qs = _stack_heads(q_ref[0] * (HEAD_DIM ** -0.5))
    s = _mm_nt(qs, k_ref[0])
    e = jnp.exp(s - jnp.max(s, axis=1, keepdims=True))
    pv = _mm(e, v_ref[0]) / jnp.sum(e, axis=1, keepdims=True)
    o_ref[0] = (_unstack_heads(pv, n) * _silu(g_ref[0].astype(F32))).astype(o_ref.dtype)


def _ctx_attention(na_ctx):
    b, lc, _ = na_ctx.shape
    tq = 64
    qrow = lambda c: (lambda i, j: (i, j, c))
    full = lambda c: (lambda i, j: (i, 0, c))
    return pl.pallas_call(
        _ctx_attn_kernel,
        grid=(b, lc // tq),
        in_specs=[pl.BlockSpec((1, tq, BR_W), qrow(0)),
                  pl.BlockSpec((1, lc, BR_W), full(1)),
                  pl.BlockSpec((1, lc, BR_W), full(2)),
                  pl.BlockSpec((1, tq, BR_W), qrow(3))],
        out_specs=pl.BlockSpec((1, tq, BR_W), qrow(0)),
        out_shape=jax.ShapeDtypeStruct((b, lc, BR_W), BF16),
        compiler_params=_params("parallel", "arbitrary"),
        name="ctx_attention",
    )(na_ctx, na_ctx, na_ctx, na_ctx)


def _stack_pair(x):
    head = _head_of_lane(x.shape)
    return jnp.concatenate([jnp.where(head == 0, x, 0.0), jnp.where(head == 1, x, 0.0)], axis=0)


def _scan_chunks(chains):
    n = chains[0][0].shape[0]
    m = 2 * n
    nc = len(chains)
    rng = range(nc)
    rev = [c[7] for c in chains]
    row_n = lax.broadcasted_iota(jnp.int32, (n, LANES), 0)
    i2 = lax.broadcasted_iota(jnp.int32, (m, 2 * m), 0)
    j2 = lax.broadcasted_iota(jnp.int32, (m, 2 * m), 1) % n
    t2 = i2 % n
    same = jnp.where(i2 < n, 0, 1)
    keep = {False: j2 < t2 + same, True: j2 > t2 - same}
    ri = lax.broadcasted_iota(jnp.int32, (LANES, LANES), 0)
    ci = lax.broadcasted_iota(jnp.int32, (LANES, LANES), 1)
    eye = ri == ci
    same_head = (ri // HEAD_DIM) == (ci // HEAD_DIM)
    d = functools.partial(jnp.dot, preferred_element_type=F32)

    def running_sum(x, reverse):
        s = 1
        while s < n:
            if reverse:
                x = x + jnp.where(row_n < n - s, pltpu.roll(x, n - s, 0), 0.0)
            else:
                x = x + jnp.where(row_n >= s, pltpu.roll(x, s, 0), 0.0)
            s *= 2
        return x

    cum = [running_sum(chains[c][3], rev[c]) for c in rng]
    tot = [jnp.sum(chains[c][3], axis=0, keepdims=True) for c in rng]
    e_neg = [jnp.exp(-cum[c]) for c in rng]
    ar = [jnp.concatenate([chains[c][2] * jnp.exp(cum[c] - chains[c][3]),
                           chains[c][0] * jnp.exp(cum[c])], axis=0).astype(BF16) for c in rng]
    bk_st = [jnp.concatenate([_stack_pair((chains[c][5] * e_neg[c]).astype(BF16)),
                              _stack_pair((chains[c][4] * e_neg[c]).astype(BF16))], axis=0) for c in rng]
    v_bf = [chains[c][1].astype(BF16) for c in rng]

    g = [jnp.where(keep[rev[c]], _mm_nt(ar[c], bk_st[c]), 0.0) for c in rng]
    lhs = [jnp.concatenate([ar[c], g[c][:, m:].astype(BF16)], axis=1) for c in rng]
    rhs = [jnp.concatenate([chains[c][6].astype(BF16), _stack_pair(v_bf[c])], axis=0) for c in rng]
    xy = [d(lhs[c], rhs[c]) for c in rng]

    u = [xy[c][:n] for c in rng]
    p = [g[c][:n, :m] for c in rng]
    step = 1
    while step < n:
        step *= 2
        last = step >= n
        nxt_u, nxt_p = [], []
        for c in rng:
            p_bf = p[c].astype(BF16)
            u_hi, u_lo = _split(u[c])
            rhs_parts = [_stack_pair(u_hi)]
            if step <= SCAN_PAIR_STEPS:
                rhs_parts.append(_stack_pair(u_lo))
            if not last:
                rhs_parts.append(_stack_pair(p_bf))
            acc = d(p_bf, jnp.concatenate(rhs_parts, axis=1))
            du = acc[:, :LANES]
            if step <= SCAN_PAIR_STEPS:
                du = du + acc[:, LANES:2 * LANES]
            nxt_u.append(u[c] + du)
            nxt_p.append(None if last else acc[:, -LANES:])
        u, p = nxt_u, nxt_p

    outs = []
    for c in rng:
        u_bf = u[c].astype(BF16)
        y = xy[c][n:] + d(g[c][n:, :m].astype(BF16), _stack_pair(u_bf))
        e_rest = jnp.exp(tot[c] - cum[c])
        bk_rest = jnp.concatenate([chains[c][5] * e_rest, chains[c][4] * e_rest], axis=0)
        decay = jnp.sum(jnp.where(eye, jnp.exp(tot[c]), 0.0), axis=1, keepdims=True)
        upd = _mm_tn(bk_rest, jnp.concatenate([u_bf, v_bf[c]], axis=0))
        outs.append((y, chains[c][6] * decay + jnp.where(same_head, upd, 0.0)))
    return outs


def _scan_kernel(sf_ref, df_ref, sb_ref, db_ref, s0_ref, yf_ref, yb_ref, sfin_ref, state_ref):
    j = pl.program_id(1)

    @pl.when(j == 0)
    def _():
        state_ref[...] = s0_ref[...]

    nb = sf_ref.shape[0]
    npair = BR_W // LANES
    chains = []
    for i in range(nb):
        for dr, (s_ref, d_ref) in enumerate(((sf_ref, df_ref), (sb_ref, db_ref))):
            for p in range(npair):
                col = lambda blk: slice(blk * BR_W + p * LANES, blk * BR_W + (p + 1) * LANES)
                chains.append((s_ref[i, :, col(PREP_R)], s_ref[i, :, col(PREP_V)], s_ref[i, :, col(PREP_A)],
                               d_ref[i, :, col(0)], d_ref[i, :, col(1)], d_ref[i, :, col(2)],
                               state_ref[i, dr, p], dr == 1))
    outs = iter(_scan_chunks(chains))
    for i in range(nb):
        for dr, y_ref in enumerate((yf_ref, yb_ref)):
            for p in range(npair):
                y, new_state = next(outs)
                y_ref[i, :, p * LANES:(p + 1) * LANES] = y
                state_ref[i, dr, p] = new_state

    @pl.when(j == pl.num_programs(1) - 1)
    def _():
        sfin_ref[...] = state_ref[...]


def _rwkv_scan(prep, s0):
    b, t, _ = prep.shape
    n = t // SCAN_L
    npair = BR_W // LANES
    blk3 = 3 * BR_W
    fwd = lambda c: (lambda i, j: (i, j, c))
    bwd = lambda c: (lambda i, j: (i, n - 1 - j, c))
    st = lambda i, j: (i, 0, 0, 0, 0)
    nb = SCAN_NB if b % SCAN_NB == 0 else 1
    return pl.pallas_call(
        _scan_kernel,
        grid=(b // nb, n),
        in_specs=[pl.BlockSpec((nb, SCAN_L, blk3), fwd(0)),
                  pl.BlockSpec((nb, SCAN_L, blk3), fwd(1)),
                  pl.BlockSpec((nb, SCAN_L, blk3), bwd(0)),
                  pl.BlockSpec((nb, SCAN_L, blk3), bwd(2)),
                  pl.BlockSpec((nb, 2, npair, LANES, LANES), st)],
        out_specs=[pl.BlockSpec((nb, SCAN_L, BR_W), fwd(0)),
                   pl.BlockSpec((nb, SCAN_L, BR_W), bwd(0)),
                   pl.BlockSpec((nb, 2, npair, LANES, LANES), st)],
        out_shape=[jax.ShapeDtypeStruct((b, t, BR_W), F32),
                   jax.ShapeDtypeStruct((b, t, BR_W), F32),
                   jax.ShapeDtypeStruct((b, 2, npair, LANES, LANES), F32)],
        scratch_shapes=[pltpu.VMEM((nb, 2, npair, LANES, LANES), F32)],
        compiler_params=_params("parallel", "arbitrary"),
        name="rwkv_scan",
    )(prep, prep, prep, prep, s0)


def _rwkv_finish(y_sum, bonus, gate, gn_g, gn_b):
    avg = _group_matrix(1.0 / HEAD_DIM)
    mu = _mm_exact_rhs(y_sum, avg, precise=True)
    oc = y_sum - mu
    var = _mm_exact_rhs(oc * oc, avg)
    o = oc * lax.rsqrt(var + GN_EPS) * gn_g + gn_b + bonus
    return o * _silu(gate)


def _merge_kernel(x_ref, g_ref, sh_ref, sc_ref, gt_ref, gm_ref, na_ref, yf_ref, yb_ref, bonus_ref, rwg_ref,
                  gng_ref, gnb_ref, pl_ref, wm_ref, wbr_ref, wout_ref, fin_ref, o_ref, *, final):
    x = x_ref[0]
    d = x.shape[-1]
    h = _norm_mod(x, g_ref[0], sh_ref[0, 0], sc_ref[0, 0]).astype(BF16)
    rw = _rwkv_finish(yf_ref[0] + yb_ref[0], bonus_ref[0], rwg_ref[0].astype(F32), gng_ref[0], gnb_ref[0])
    y = None
    for i, br in enumerate((gm_ref[0], na_ref[0], rw.astype(BF16), pl_ref[0])):
        logits = jnp.dot(h, wm_ref[0, :, i * d:(i + 1) * d], preferred_element_type=F32)
        term = _sigmoid_t(logits) * jnp.dot(br, wbr_ref[0, i], preferred_element_type=F32)
        y = term if y is None else y + term
    out = x + gt_ref[0, 0] * jnp.dot(y.astype(BF16), wout_ref[0], preferred_element_type=F32)
    if final:
        ms = jnp.mean(out * out, axis=-1, keepdims=True)
        out = out * lax.rsqrt(ms + EPS) * fin_ref[...]
    o_ref[0] = out


def _merge(x, layer, mod, mod_row, gm, na, yf, yb, prep, rwg, plb, p, final):
    b, t, d = x.shape
    tm = min(t, 512)
    row = lambda i, j: (i, j, 0)
    modspec = lambda c: pl.BlockSpec((1, 1, 1, d), lambda i, j: (layer, mod_row(i), 0, c))
    br = pl.BlockSpec((1, tm, BR_W), row)
    return pl.pallas_call(
        functools.partial(_merge_kernel, final=final),
        grid=(b, t // tm),
        in_specs=[pl.BlockSpec((1, tm, d), row), _layer_spec((1, d), layer), modspec(0), modspec(1), modspec(2),
                  br, br, br, br, pl.BlockSpec((1, tm, BR_W), lambda i, j: (i, j, PREP_BONUS)), br,
                  _layer_spec((1, BR_W), layer), _layer_spec((1, BR_W), layer), br,
                  _layer_spec((d, N_BRANCH * d), layer), _layer_spec((N_BRANCH, BR_W, d), layer),
                  _layer_spec((d, d), layer), pl.BlockSpec((1, d), lambda i, j: (0, 0))],
        out_specs=pl.BlockSpec((1, tm, d), row),
        out_shape=jax.ShapeDtypeStruct((b, t, d), F32),
        compiler_params=_params("parallel", "parallel"),
        name="merge_out_proj",
    )(x, p['norm_g'], mod, mod, mod, gm, na, yf, yb, prep, rwg, p['rw_gn_g'], p['rw_gn_b'], plb,
      p['w_merge'], p['w_br'], p['w_out'], p['final_g'])


def _layer(x, ctx, layer, mod, p, last):
    b = x.shape[0]
    latent_row = lambda i: i
    ctx_row = lambda i: b

    gm_c, na_c, prep_c, rwg_c, pl_c = _project_mix(ctx, layer, mod, ctx_row, p)
    gm_x, na_x, prep_x, rwg_x, pl_x = _project_mix(x, layer, mod, latent_row, p)

    zero_state = jnp.zeros((b, 2, BR_W // LANES, LANES, LANES), F32)
    yf_c, yb_c, s_ctx = _rwkv_scan(prep_c, zero_state)
    yf_x, yb_x, _ = _rwkv_scan(prep_x, s_ctx)

    x_new = _merge(x, layer, mod, latent_row, gm_x, _natten(na_x, na_c, layer, p['na_bias']),
                   yf_x, yb_x, prep_x, rwg_x, _pool(pl_x, layer, p), p, final=last)
    if last:
        return x_new, ctx
    ctx_new = _merge(ctx, layer, mod, ctx_row, gm_c, _ctx_attention(na_c),
                     yf_c, yb_c, prep_c, rwg_c, _pool(pl_c, layer, p), p, final=False)
    return x_new, ctx_new


def kernel(x, c, ctx, c_ctx, ada_w, ada_b, norm_g, w_in, gm_ln_g, gm_ln_b, gm_ws, gm_bs, na_rpb, rw_mu, rw_w0, rw_w2, rw_a0, rw_a2, rw_kk, rw_ka, rw_rk, rw_gn_g, rw_gn_b, pl_w, pl_scale, w_br, w_out, final_g):
    b, _, d = x.shape
    depth = ada_w.shape[0]
    mod_rows = 16
    cs = jnp.zeros((mod_rows, d), F32).at[:b].set(c).at[b].set(c_ctx)
    mod = _modulation(cs, ada_w, ada_b).reshape(depth, mod_rows, 1, 3 * d)

    vec = lambda a: a.reshape(depth, 1, -1)
    lora_zeros = jnp.zeros((depth, LORA, 2 * BR_W), F32)
    side = lambda w: jnp.concatenate([w[:, 0], w[:, 1]], axis=2)
    lora = jnp.concatenate([jnp.concatenate([side(rw_w2), lora_zeros], axis=2),
                            jnp.concatenate([lora_zeros, side(rw_a2)], axis=2)], axis=1).astype(BF16)
    p = {
        'norm_g': vec(norm_g), 'final_g': final_g.reshape(1, d),
        'w_mix': w_in[:, :, :MIX_COLS].astype(BF16), 'w_merge': w_in[:, :, MIX_COLS:].astype(BF16),
        'w_br': w_br.astype(BF16), 'w_out': w_out.astype(BF16),
        'gm_ln_g': vec(gm_ln_g), 'gm_ln_b': vec(gm_ln_b), 'gm_ws': gm_ws.astype(BF16),
        'gm_bs': jnp.repeat(gm_bs.transpose(0, 2, 1), HEAD_DIM, axis=2),
        'na_bias': _natten_bias(na_rpb),
        'rw_mu': vec(rw_mu), 'rw_kk': vec(rw_kk), 'rw_ka': vec(rw_ka), 'rw_rk': vec(rw_rk),
        'rw_w0': rw_w0, 'rw_a0': rw_a0,
        'rw_lora': jnp.concatenate([lora, lora], axis=1),
        'rw_gn_g': vec(rw_gn_g), 'rw_gn_b': vec(rw_gn_b),
        'pl_w': jnp.einsum('lgcd,gh->lgchd', pl_w, jnp.eye(N_HEADS, dtype=F32)).reshape(depth, BR_W, BR_W).astype(BF16),
        'pl_scale': vec(pl_scale),
    }
    for layer in range(depth):
        x, ctx = _layer(x, ctx, layer, mod, p, last=(layer == depth - 1))
    return x
```

```python
import functools

import jax
import jax.numpy as jnp
import numpy as np
from jax import lax
from jax.experimental import pallas as pl
from jax.experimental.pallas import tpu as pltpu

F32 = jnp.float32
BF16 = jnp.bfloat16

N_HEADS = 4
N_BRANCH = 4
HEAD_DIM = 64
BR_W = N_HEADS * HEAD_DIM
LANES = 128
GRID_W = 64
NA_ROWS = 8
NA_COLS = 16
NA_RT = 16
CHUNK = 128
LORA = 64
POOL_WINDOWS = (2, 4, 8, 16)
POOL_HALO = 16
SCAN_L = 64
SCAN_PAIR_STEPS = 16
SCAN_NB = 4
EPS = 1e-6
GN_EPS = 64e-5
NEG = -1e30
VMEM_LIMIT = 56 * 1024 * 1024

SEG_GM, SEG_NA, SEG_RWIN, SEG_RWG, SEG_PL = 3 * BR_W, 4 * BR_W, 3 * BR_W + 2 * LORA, BR_W, 2 * BR_W
MIX_COLS = SEG_GM + SEG_NA + SEG_RWIN + SEG_RWG + SEG_PL


def _params(*sem):
    return pltpu.CompilerParams(dimension_semantics=sem, vmem_limit_bytes=VMEM_LIMIT)


def _sigmoid(x):
    return 1.0 / (1.0 + jnp.exp(-x))


def _sigmoid_t(x):
    return 0.5 * jnp.tanh(0.5 * x) + 0.5


def _silu(x):
    return x * _sigmoid_t(x)


def _gelu(x):
    return 0.5 * x * (1.0 + jnp.tanh(0.7978845608028654 * (x + 0.044715 * x * x * x)))


def _mm(a, b):
    return jnp.dot(a.astype(BF16), b.astype(BF16), preferred_element_type=F32)


def _mm_nt(a, b):
    return lax.dot_general(a.astype(BF16), b.astype(BF16), (((1,), (1,)), ((), ())),
                           preferred_element_type=F32)


def _mm_tn(a, b):
    return lax.dot_general(a.astype(BF16), b.astype(BF16), (((0,), (0,)), ((), ())),
                           preferred_element_type=F32)


def _split(x):
    hi = x.astype(BF16)
    lo = (x - hi.astype(F32)).astype(BF16)
    return hi, lo


def _mm3(a, b):
    ah, al = _split(a)
    bh, bl = _split(b)
    d = functools.partial(jnp.dot, preferred_element_type=F32)
    return d(ah, bh) + d(ah, bl) + d(al, bh)


def _mm_exact_rhs(a, b_bf16, precise=False):
    d = functools.partial(jnp.dot, preferred_element_type=F32)
    if not precise:
        return d(a.astype(BF16), b_bf16)
    a_hi, a_lo = _split(a)
    return d(a_hi, b_bf16) + d(a_lo, b_bf16)


def _group_matrix(value):
    i = lax.broadcasted_iota(jnp.int32, (BR_W, BR_W), 0) // HEAD_DIM
    j = lax.broadcasted_iota(jnp.int32, (BR_W, BR_W), 1) // HEAD_DIM
    return jnp.where(i == j, value, 0.0).astype(BF16)


def _head_of_lane(shape):
    return lax.broadcasted_iota(jnp.int32, shape, len(shape) - 1) // HEAD_DIM


def _norm_mod(x, g, shift, scale):
    ms = jnp.mean(x * x, axis=-1, keepdims=True)
    return (x * lax.rsqrt(ms + EPS) * g) * (1.0 + scale) + shift


def _mod_kernel(cs_ref, w_ref, b_ref, o_ref):
    o_ref[0] = _mm3(_silu(cs_ref[...]), w_ref[0]) + b_ref[0]


def _modulation(cs, ada_w, ada_b):
    depth, d, d3 = ada_w.shape
    rows = cs.shape[0]
    tn = 768
    return pl.pallas_call(
        _mod_kernel,
        grid=(depth, d3 // tn),
        in_specs=[pl.BlockSpec((rows, d), lambda l, j: (0, 0)),
                  pl.BlockSpec((1, d, tn), lambda l, j: (l, 0, j)),
                  pl.BlockSpec((1, 1, tn), lambda l, j: (l, 0, j))],
        out_specs=pl.BlockSpec((1, rows, tn), lambda l, j: (l, 0, j)),
        out_shape=jax.ShapeDtypeStruct((depth, rows, d3), F32),
        compiler_params=_params("arbitrary", "arbitrary"),
        name="adaln_modulation",
    )(cs, ada_w, ada_b.reshape(depth, 1, d3))


def _layer_spec(tail, layer):
    zeros = (0,) * len(tail)
    return pl.BlockSpec((1,) + tuple(tail), lambda i, j: (layer,) + zeros)


def _gmlp_mix(z, lng, lnb, ws_ref, bs, o_ref, row0):
    tm = z.shape[0]
    u = _gelu(z[:, 0:BR_W])
    v = _gelu(z[:, BR_W:2 * BR_W])
    gate = z[:, 2 * BR_W:3 * BR_W]
    avg = _group_matrix(1.0 / HEAD_DIM)
    mu = _mm_exact_rhs(v, avg, precise=True)
    vc = v - mu
    var = _mm_exact_rhs(vc * vc, avg)
    vh = (vc * lax.rsqrt(var + EPS) * lng + lnb).astype(BF16)
    su = u * _silu(gate)
    for c in range(tm // CHUNK):
        rows = slice(c * CHUNK, (c + 1) * CHUNK)
        mixed = jnp.dot(ws_ref[0], _stack_heads(vh[rows]), preferred_element_type=F32)
        o_ref[0, row0 + c * CHUNK:row0 + (c + 1) * CHUNK, :] = (su[rows] * (mixed + bs)).astype(o_ref.dtype)


PREP_R, PREP_V, PREP_A = 0, 1, 2
PREP_DIR = 3
PREP_BONUS = 9
PREP_COLS = 10 * BR_W
DECAY_SCALE = 0.6065306597126334


def _token_shift(z, halo, mu):
    tm = z.shape[0]
    n = tm + 8
    ze = jnp.concatenate([z, halo], axis=0)
    neighbours = pltpu.roll(ze, 1, 0)[0:tm] + pltpu.roll(ze, n - 1, 0)[0:tm]
    return z * (1.0 - mu) + neighbours * (0.5 * mu)


def _rwkv_prepare(zf, kkw, ka, rk, w0, a0, lora_ref, o_ref, rows):
    r = zf[:, 0:BR_W]
    k = zf[:, BR_W:2 * BR_W]
    v = zf[:, 2 * BR_W:3 * BR_W]
    lo = zf[:, 3 * BR_W:3 * BR_W + 2 * LORA]
    ones = _group_matrix(1.0)
    kk = k * kkw
    kk = kk / jnp.maximum(jnp.sqrt(_mm_exact_rhs(kk * kk, ones)), 1e-12)
    o_ref[0, rows, PREP_R * BR_W:(PREP_R + 1) * BR_W] = r
    o_ref[0, rows, PREP_V * BR_W:(PREP_V + 1) * BR_W] = v
    o_ref[0, rows, PREP_A * BR_W:(PREP_A + 1) * BR_W] = -kk
    decay_lane = lax.broadcasted_iota(jnp.int32, lo.shape, 1) < LORA
    t_hi, t_lo = _split(jnp.where(decay_lane, jnp.tanh(lo), lo))
    low_rank = jnp.dot(jnp.concatenate([t_hi, t_lo], axis=1), lora_ref[0], preferred_element_type=F32)
    kd_sum = jnp.zeros_like(k)
    for d in range(2):
        y = w0[d:d + 1, :] + low_rank[:, d * BR_W:(d + 1) * BR_W]
        a = _sigmoid(a0[d:d + 1, :] + low_rank[:, (2 + d) * BR_W:(3 + d) * BR_W])
        kd = k * (1.0 + (a - 1.0) * ka)
        c0 = (PREP_DIR + 3 * d) * BR_W
        o_ref[0, rows, c0:c0 + BR_W] = _sigmoid(y) * (-DECAY_SCALE)
        o_ref[0, rows, c0 + BR_W:c0 + 2 * BR_W] = kd
        o_ref[0, rows, c0 + 2 * BR_W:c0 + 3 * BR_W] = kk * a
        kd_sum = kd_sum + kd
    o_ref[0, rows, PREP_BONUS * BR_W:] = _mm_exact_rhs(r * kd_sum * rk, ones) * v


OFF_GM = 0
OFF_NA = OFF_GM + SEG_GM
OFF_RWIN = OFF_NA + SEG_NA
OFF_RWG = OFF_RWIN + SEG_RWIN
OFF_PL = OFF_RWG + SEG_RWG


def _projmix_kernel(x_ref, xp_ref, xn_ref, ng_ref, sh_ref, sc_ref, w_ref,
                    lng_ref, lnb_ref, ws_ref, bs_ref,
                    mu_ref, kkw_ref, ka_ref, rk_ref, w0_ref, a0_ref, lora_ref,
                    gm_ref, na_ref, prep_ref, rwg_ref, pl_ref):
    j = pl.program_id(1)
    tm = x_ref.shape[1]
    g, sh, sc = ng_ref[0], sh_ref[0, 0], sc_ref[0, 0]
    h = _norm_mod(x_ref[0], g, sh, sc).astype(BF16)
    proj = lambda lhs, off, width: jnp.dot(lhs, w_ref[0, :, off:off + width], preferred_element_type=F32)

    xh = jnp.concatenate([xn_ref[0], xp_ref[0]], axis=0)
    z_rw = proj(jnp.concatenate([h, _norm_mod(xh, g, sh, sc).astype(BF16)], axis=0), OFF_RWIN, SEG_RWIN)
    row = lax.broadcasted_iota(jnp.int32, (8, SEG_RWIN), 0)
    has_prev = jnp.where(j > 0, 1.0, 0.0)
    has_next = jnp.where(j < pl.num_programs(1) - 1, 1.0, 0.0)
    halo = (jnp.where(row == 0, z_rw[tm:tm + 8], 0.0) * has_next
            + jnp.where(row == 7, z_rw[tm + 8:tm + 16], 0.0) * has_prev)
    zf = _token_shift(z_rw[0:tm], halo, mu_ref[0])

    z_gm = proj(h, OFF_GM, SEG_GM)
    _rwkv_prepare(zf, kkw_ref[0], ka_ref[0], rk_ref[0], w0_ref[0], a0_ref[0], lora_ref, prep_ref, slice(None))
    na_ref[0] = proj(h, OFF_NA, SEG_NA).astype(na_ref.dtype)
    rwg_ref[0] = proj(h, OFF_RWG, SEG_RWG).astype(rwg_ref.dtype)
    pl_ref[0] = proj(h, OFF_PL, SEG_PL).astype(pl_ref.dtype)
    _gmlp_mix(z_gm, lng_ref[0], lnb_ref[0], ws_ref, bs_ref[0], gm_ref, 0)


def _project_mix(x, layer, mod, mod_row, p):
    b, t, d = x.shape
    tm = min(t, 512)
    nblk8 = t // 8
    row = lambda i, j: (i, j, 0)
    modspec = lambda c: pl.BlockSpec((1, 1, 1, d), lambda i, j: (layer, mod_row(i), 0, c))
    widths = (BR_W, SEG_NA, PREP_COLS, SEG_RWG, SEG_PL)
    dtypes = (BF16, BF16, F32, BF16, BF16)
    return pl.pallas_call(
        _projmix_kernel,
        grid=(b, t // tm),
        in_specs=[pl.BlockSpec((1, tm, d), row),
                  pl.BlockSpec((1, 8, d), lambda i, j: (i, jnp.maximum(j * (tm // 8) - 1, 0), 0)),
                  pl.BlockSpec((1, 8, d), lambda i, j: (i, jnp.minimum((j + 1) * (tm // 8), nblk8 - 1), 0)),
                  _layer_spec((1, d), layer), modspec(0), modspec(1),
                  _layer_spec((d, MIX_COLS), layer),
                  _layer_spec((1, BR_W), layer), _layer_spec((1, BR_W), layer),
                  _layer_spec((CHUNK, N_HEADS * CHUNK), layer), _layer_spec((CHUNK, BR_W), layer),
                  _layer_spec((1, SEG_RWIN), layer), _layer_spec((1, BR_W), layer),
                  _layer_spec((1, BR_W), layer), _layer_spec((1, BR_W), layer),
                  _layer_spec((2, BR_W), layer), _layer_spec((2, BR_W), layer),
                  _layer_spec((4 * LORA, 4 * BR_W), layer)],
        out_specs=[pl.BlockSpec((1, tm, w), row) for w in widths],
        out_shape=[jax.ShapeDtypeStruct((b, t, w), dt) for w, dt in zip(widths, dtypes)],
        compiler_params=_params("parallel", "parallel"),
        name="norm_in_proj_mix",
    )(x, x, x, p['norm_g'], mod, mod, p['w_mix'],
      p['gm_ln_g'], p['gm_ln_b'], p['gm_ws'], p['gm_bs'],
      p['rw_mu'], p['rw_kk'], p['rw_ka'], p['rw_rk'], p['rw_w0'], p['rw_a0'], p['rw_lora'])


def _pool_kernel(p_ref, g_ref, w_ref, sc_ref, o_ref, ext_ref, *, seq_len):
    j = pl.program_id(1)
    tm = o_ref.shape[1]
    n = tm + 2 * POOL_HALO

    @pl.when(j == 0)
    def _():
        ext_ref[0:POOL_HALO, :] = jnp.zeros((POOL_HALO, BR_W), F32)
        ext_ref[POOL_HALO + seq_len:, :] = jnp.zeros((POOL_HALO, BR_W), F32)
        ext_ref[POOL_HALO:POOL_HALO + seq_len, :] = p_ref[0].astype(F32)

    t0 = pl.multiple_of(j * tm, 8)
    x = ext_ref[pl.ds(t0, n), :]
    s2 = x + pltpu.roll(x, 1, 0)
    s4 = pltpu.roll(s2, 1, 0) + pltpu.roll(s2, n - 1, 0)
    s8 = pltpu.roll(s4, 2, 0) + pltpu.roll(s4, n - 2, 0)
    s16 = pltpu.roll(s8, 4, 0) + pltpu.roll(s8, n - 4, 0)
    mid = slice(POOL_HALO, POOL_HALO + tm)
    grp = _head_of_lane((tm, BR_W))
    ssum = jnp.where(grp == 0, s2[mid], jnp.where(grp == 1, s4[mid], jnp.where(grp == 2, s8[mid], s16[mid])))
    half = jnp.where(grp == 0, 1, jnp.where(grp == 1, 2, jnp.where(grp == 2, 4, 8)))
    t = t0 + lax.broadcasted_iota(jnp.int32, (tm, BR_W), 0)
    cnt = jnp.minimum(t + half, seq_len) - jnp.maximum(t - half, 0)
    d = ssum / cnt.astype(F32) - x[mid]
    y = _mm(d, w_ref[0])
    o_ref[0] = (y * sc_ref[0] * _silu(g_ref[0].astype(F32))).astype(o_ref.dtype)


def _pool(plz, layer, p):
    b, t, _ = plz.shape
    tm = min(t, 512)
    return pl.pallas_call(
        functools.partial(_pool_kernel, seq_len=t),
        grid=(b, t // tm),
        in_specs=[pl.BlockSpec((1, t, BR_W), lambda i, j: (i, 0, 0)),
                  pl.BlockSpec((1, tm, BR_W), lambda i, j: (i, j, 1)),
                  _layer_spec((BR_W, BR_W), layer),
                  _layer_spec((1, BR_W), layer)],
        out_specs=pl.BlockSpec((1, tm, BR_W), lambda i, j: (i, j, 0)),
        out_shape=jax.ShapeDtypeStruct((b, t, BR_W), BF16),
        scratch_shapes=[pltpu.VMEM((t + 2 * POOL_HALO, BR_W), F32)],
        compiler_params=_params("parallel", "arbitrary"),
        name="pool_mixer",
    )(plz, plz, p['pl_w'], p['pl_scale'])


def _stack_heads(q):
    head = _head_of_lane(q.shape)
    return jnp.concatenate([jnp.where(head == h, q, 0.0) for h in range(N_HEADS)], axis=0)


def _unstack_heads(o, n):
    head = _head_of_lane((n, BR_W))
    out = jnp.zeros((n, BR_W), F32)
    for h in range(N_HEADS):
        out = jnp.where(head == h, o[h * n:(h + 1) * n], out)
    return out


def _natten_kernel(q_ref, k_ref, v_ref, g_ref, kc_ref, vc_ref, bias_ref, o_ref, *, rows):
    band = NA_ROWS * GRID_W
    rng = range(NA_RT)
    kc = kc_ref[0]
    vc = vc_ref[0]
    r = [pl.program_id(1) * NA_RT + i for i in rng]
    rs = [jnp.clip(r[i] - NA_ROWS // 2, 0, rows - NA_ROWS) for i in rng]
    start = [pl.multiple_of(rs[i] * GRID_W, GRID_W) for i in rng]
    qs = [_stack_heads(q_ref[0, i * GRID_W:(i + 1) * GRID_W, :] * (HEAD_DIM ** -0.5)) for i in rng]
    kb = [k_ref[0, pl.ds(start[i], band), :] for i in rng]
    vb = [v_ref[0, pl.ds(start[i], band), :] for i in rng]
    s_b = [_mm_nt(qs[i], kb[i]) + bias_ref[0, rs[i] - r[i] + NA_ROWS - 1] for i in rng]
    s_c = [_mm_nt(qs[i], kc) for i in rng]
    m = [jnp.maximum(jnp.max(s_b[i], axis=1, keepdims=True), jnp.max(s_c[i], axis=1, keepdims=True)) for i in rng]
    e_b = [jnp.exp(s_b[i] - m[i]) for i in rng]
    e_c = [jnp.exp(s_c[i] - m[i]) for i in rng]
    den = [jnp.sum(e_b[i], axis=1, keepdims=True) + jnp.sum(e_c[i], axis=1, keepdims=True) for i in rng]
    pv = [(_mm(e_b[i], vb[i]) + _mm(e_c[i], vc)) / den[i] for i in rng]
    for i in rng:
        rows_i = slice(i * GRID_W, (i + 1) * GRID_W)
        gate = _silu(g_ref[0, rows_i, :].astype(F32))
        o_ref[0, rows_i, :] = (_unstack_heads(pv[i], GRID_W) * gate).astype(o_ref.dtype)


def _natten_bias(rpb):
    depth = rpb.shape[0]
    c = np.arange(GRID_W)
    c_start = np.clip(c - NA_COLS // 2, 0, GRID_W - NA_COLS)
    col_ok = (c[None, :] >= c_start[:, None]) & (c[None, :] < c_start[:, None] + NA_COLS)
    dc = np.clip(c[None, :] - c[:, None], -(NA_COLS - 1), NA_COLS - 1) + NA_COLS - 1
    select = (np.arange(2 * NA_COLS - 1)[:, None, None] == dc[None]).astype(np.float32)
    rows = jnp.stack([rpb[:, :, o:o + NA_ROWS] for o in range(NA_ROWS)], axis=1)
    tab = jnp.einsum('lohic,cqk->lohqik', rows, jnp.asarray(select), precision=lax.Precision.HIGHEST)
    tab = jnp.where(jnp.asarray(col_ok)[None, None, None, :, None, :], tab, NEG)
    return tab.reshape(depth, NA_ROWS, N_HEADS * GRID_W, NA_ROWS * GRID_W)


def _natten(na, na_ctx, layer, bias):
    b, t, _ = na.shape
    lc = na_ctx.shape[1]
    rows = t // GRID_W
    band = NA_ROWS * GRID_W
    tq = NA_RT * GRID_W
    qrow = lambda c: (lambda i, r: (i, r, c))
    full = lambda c: (lambda i, r: (i, 0, c))
    return pl.pallas_call(
        functools.partial(_natten_kernel, rows=rows),
        grid=(b, rows // NA_RT),
        in_specs=[pl.BlockSpec((1, tq, BR_W), qrow(0)),
                  pl.BlockSpec((1, t, BR_W), full(1)),
                  pl.BlockSpec((1, t, BR_W), full(2)),
                  pl.BlockSpec((1, tq, BR_W), qrow(3)),
                  pl.BlockSpec((1, lc, BR_W), full(1)),
                  pl.BlockSpec((1, lc, BR_W), full(2)),
                  _layer_spec((NA_ROWS, N_HEADS * GRID_W, band), layer)],
        out_specs=pl.BlockSpec((1, tq, BR_W), qrow(0)),
        out_shape=jax.ShapeDtypeStruct((b, t, BR_W), BF16),
        compiler_params=_params("parallel", "arbitrary"),
        name="natten_mixer",
    )(na, na, na, na, na_ctx, na_ctx, bias)


def _ctx_attn_kernel(q_ref, k_ref, v_ref, g_ref, o_ref):
    n = q_ref.shape[1]
    qs = _stack_heads(q_ref[0] * (HEAD_DIM ** -0.5))
    s = _mm_nt(qs, k_ref[0])
    e = jnp.exp(s - jnp.max(s, axis=1, keepdims=True))
    pv = _mm(e, v_ref[0]) / jnp.sum(e, axis=1, keepdims=True)
    o_ref[0] = (_unstack_heads(pv, n) * _silu(g_ref[0].astype(F32))).astype(o_ref.dtype)


def _ctx_attention(na_ctx):
    b, lc, _ = na_ctx.shape
    tq = 64
    qrow = lambda c: (lambda i, j: (i, j, c))
    full = lambda c: (lambda i, j: (i, 0, c))
    return pl.pallas_call(
        _ctx_attn_kernel,
        grid=(b, lc // tq),
        in_specs=[pl.BlockSpec((1, tq, BR_W), qrow(0)),
                  pl.BlockSpec((1, lc, BR_W), full(1)),
                  pl.BlockSpec((1, lc, BR_W), full(2)),
                  pl.BlockSpec((1, tq, BR_W), qrow(3))],
        out_specs=pl.BlockSpec((1, tq, BR_W), qrow(0)),
        out_shape=jax.ShapeDtypeStruct((b, lc, BR_W), BF16),
        compiler_params=_params("parallel", "arbitrary"),
        name="ctx_attention",
    )(na_ctx, na_ctx, na_ctx, na_ctx)


def _stack_pair(x):
    head = _head_of_lane(x.shape)
    return jnp.concatenate([jnp.where(head == 0, x, 0.0), jnp.where(head == 1, x, 0.0)], axis=0)


def _scan_chunks(chains):
    n = chains[0][0].shape[0]
    m = 2 * n
    nc = len(chains)
    rng = range(nc)
    rev = [c[7] for c in chains]
    row_n = lax.broadcasted_iota(jnp.int32, (n, LANES), 0)
    i2 = lax.broadcasted_iota(jnp.int32, (m, 2 * m), 0)
    j2 = lax.broadcasted_iota(jnp.int32, (m, 2 * m), 1) % n
    t2 = i2 % n
    same = jnp.where(i2 < n, 0, 1)
    keep = {False: j2 < t2 + same, True: j2 > t2 - same}
    ri = lax.broadcasted_iota(jnp.int32, (LANES, LANES), 0)
    ci = lax.broadcasted_iota(jnp.int32, (LANES, LANES), 1)
    eye = ri == ci
    same_head = (ri // HEAD_DIM) == (ci // HEAD_DIM)
    d = functools.partial(jnp.dot, preferred_element_type=F32)

    def running_sum(x, reverse):
        s = 1
        while s < n:
            if reverse:
                x = x + jnp.where(row_n < n - s, pltpu.roll(x, n - s, 0), 0.0)
            else:
                x = x + jnp.where(row_n >= s, pltpu.roll(x, s, 0), 0.0)
            s *= 2
        return x

    cum = [running_sum(chains[c][3], rev[c]) for c in rng]
    tot = [jnp.sum(chains[c][3], axis=0, keepdims=True) for c in rng]
    e_neg = [jnp.exp(-cum[c]) for c in rng]
    ar = [jnp.concatenate([chains[c][2] * jnp.exp(cum[c] - chains[c][3]),
                           chains[c][0] * jnp.exp(cum[c])], axis=0).astype(BF16) for c in rng]
    bk_st = [jnp.concatenate([_stack_pair((chains[c][5] * e_neg[c]).astype(BF16)),
                              _stack_pair((chains[c][4] * e_neg[c]).astype(BF16))], axis=0) for c in rng]
    v_bf = [chains[c][1].astype(BF16) for c in rng]

    g = [jnp.where(keep[rev[c]], _mm_nt(ar[c], bk_st[c]), 0.0) for c in rng]
    lhs = [jnp.concatenate([ar[c], g[c][:, m:].astype(BF16)], axis=1) for c in rng]
    rhs = [jnp.concatenate([chains[c][6].astype(BF16), _stack_pair(v_bf[c])], axis=0) for c in rng]
    xy = [d(lhs[c], rhs[c]) for c in rng]

    u = [xy[c][:n] for c in rng]
    p = [g[c][:n, :m] for c in rng]
    step = 1
    while step < n:
        step *= 2
        last = step >= n
        nxt_u, nxt_p = [], []
        for c in rng:
            p_bf = p[c].astype(BF16)
            u_hi, u_lo = _split(u[c])
            rhs_parts = [_stack_pair(u_hi)]
            if step <= SCAN_PAIR_STEPS:
                rhs_parts.append(_stack_pair(u_lo))
            if not last:
                rhs_parts.append(_stack_pair(p_bf))
            acc = d(p_bf, jnp.concatenate(rhs_parts, axis=1))
            du = acc[:, :LANES]
            if step <= SCAN_PAIR_STEPS:
                du = du + acc[:, LANES:2 * LANES]
            nxt_u.append(u[c] + du)
            nxt_p.append(None if last else acc[:, -LANES:])
        u, p = nxt_u, nxt_p

    outs = []
    for c in rng:
        u_bf = u[c].astype(BF16)
        y = xy[c][n:] + d(g[c][n:, :m].astype(BF16), _stack_pair(u_bf))
        e_rest = jnp.exp(tot[c] - cum[c])
        bk_rest = jnp.concatenate([chains[c][5] * e_rest, chains[c][4] * e_rest], axis=0)
        decay = jnp.sum(jnp.where(eye, jnp.exp(tot[c]), 0.0), axis=1, keepdims=True)
        upd = _mm_tn(bk_rest, jnp.concatenate([u_bf, v_bf[c]], axis=0))
        outs.append((y, chains[c][6] * decay + jnp.where(same_head, upd, 0.0)))
    return outs


def _scan_kernel(sf_ref, df_ref, sb_ref, db_ref, s0_ref, yf_ref, yb_ref, sfin_ref, state_ref):
    j = pl.program_id(1)

    @pl.when(j == 0)
    def _():
        state_ref[...] = s0_ref[...]

    nb = sf_ref.shape[0]
    npair = BR_W // LANES
    chains = []
    for i in range(nb):
        for dr, (s_ref, d_ref) in enumerate(((sf_ref, df_ref), (sb_ref, db_ref))):
            for p in range(npair):
                col = lambda blk: slice(blk * BR_W + p * LANES, blk * BR_W + (p + 1) * LANES)
                chains.append((s_ref[i, :, col(PREP_R)], s_ref[i, :, col(PREP_V)], s_ref[i, :, col(PREP_A)],
                               d_ref[i, :, col(0)], d_ref[i, :, col(1)], d_ref[i, :, col(2)],
                               state_ref[i, dr, p], dr == 1))
    outs = iter(_scan_chunks(chains))
    for i in range(nb):
        for dr, y_ref in enumerate((yf_ref, yb_ref)):
            for p in range(npair):
                y, new_state = next(outs)
                y_ref[i, :, p * LANES:(p + 1) * LANES] = y
                state_ref[i, dr, p] = new_state

    @pl.when(j == pl.num_programs(1) - 1)
    def _():
        sfin_ref[...] = state_ref[...]


def _rwkv_scan(prep, s0):
    b, t, _ = prep.shape
    n = t // SCAN_L
    npair = BR_W // LANES
    blk3 = 3 * BR_W
    fwd = lambda c: (lambda i, j: (i, j, c))
    bwd = lambda c: (lambda i, j: (i, n - 1 - j, c))
    st = lambda i, j: (i, 0, 0, 0, 0)
    nb = SCAN_NB if b % SCAN_NB == 0 else 1
    return pl.pallas_call(
        _scan_kernel,
        grid=(b // nb, n),
        in_specs=[pl.BlockSpec((nb, SCAN_L, blk3), fwd(0)),
                  pl.BlockSpec((nb, SCAN_L, blk3), fwd(1)),
                  pl.BlockSpec((nb, SCAN_L, blk3), bwd(0)),
                  pl.BlockSpec((nb, SCAN_L, blk3), bwd(2)),
                  pl.BlockSpec((nb, 2, npair, LANES, LANES), st)],
        out_specs=[pl.BlockSpec((nb, SCAN_L, BR_W), fwd(0)),
                   pl.BlockSpec((nb, SCAN_L, BR_W), bwd(0)),
                   pl.BlockSpec((nb, 2, npair, LANES, LANES), st)],
        out_shape=[jax.ShapeDtypeStruct((b, t, BR_W), F32),
                   jax.ShapeDtypeStruct((b, t, BR_W), F32),
                   jax.ShapeDtypeStruct((b, 2, npair, LANES, LANES), F32)],
        scratch_shapes=[pltpu.VMEM((nb, 2, npair, LANES, LANES), F32)],
        compiler_params=_params("parallel", "arbitrary"),
        name="rwkv_scan",
    )(prep, prep, prep, prep, s0)


def _rwkv_finish(y_sum, bonus, gate, gn_g, gn_b):
    avg = _group_matrix(1.0 / HEAD_DIM)
    mu = _mm_exact_rhs(y_sum, avg, precise=True)
    oc = y_sum - mu
    var = _mm_exact_rhs(oc * oc, avg)
    o = oc * lax.rsqrt(var + GN_EPS) * gn_g + gn_b + bonus
    return o * _silu(gate)


def _merge_kernel(x_ref, g_ref, sh_ref, sc_ref, gt_ref, gm_ref, na_ref, yf_ref, yb_ref, bonus_ref, rwg_ref,
                  gng_ref, gnb_ref, pl_ref, wm_ref, wbr_ref, wout_ref, fin_ref, o_ref, *, final):
    x = x_ref[0]
    d = x.shape[-1]
    h = _norm_mod(x, g_ref[0], sh_ref[0, 0], sc_ref[0, 0]).astype(BF16)
    rw = _rwkv_finish(yf_ref[0] + yb_ref[0], bonus_ref[0], rwg_ref[0].astype(F32), gng_ref[0], gnb_ref[0])
    y = None
    for i, br in enumerate((gm_ref[0], na_ref[0], rw.astype(BF16), pl_ref[0])):
        logits = jnp.dot(h, wm_ref[0, :, i * d:(i + 1) * d], preferred_element_type=F32)
        term = _sigmoid_t(logits) * jnp.dot(br, wbr_ref[0, i], preferred_element_type=F32)
        y = term if y is None else y + term
    out = x + gt_ref[0, 0] * jnp.dot(y.astype(BF16), wout_ref[0], preferred_element_type=F32)
    if final:
        ms = jnp.mean(out * out, axis=-1, keepdims=True)
        out = out * lax.rsqrt(ms + EPS) * fin_ref[...]
    o_ref[0] = out


def _merge(x, layer, mod, mod_row, gm, na, yf, yb, prep, rwg, plb, p, final):
    b, t, d = x.shape
    tm = min(t, 512)
    row = lambda i, j: (i, j, 0)
    modspec = lambda c: pl.BlockSpec((1, 1, 1, d), lambda i, j: (layer, mod_row(i), 0, c))
    br = pl.BlockSpec((1, tm, BR_W), row)
    return pl.pallas_call(
        functools.partial(_merge_kernel, final=final),
        grid=(b, t // tm),
        in_specs=[pl.BlockSpec((1, tm, d), row), _layer_spec((1, d), layer), modspec(0), modspec(1), modspec(2),
                  br, br, br, br, pl.BlockSpec((1, tm, BR_W), lambda i, j: (i, j, PREP_BONUS)), br,
                  _layer_spec((1, BR_W), layer), _layer_spec((1, BR_W), layer), br,
                  _layer_spec((d, N_BRANCH * d), layer), _layer_spec((N_BRANCH, BR_W, d), layer),
                  _layer_spec((d, d), layer), pl.BlockSpec((1, d), lambda i, j: (0, 0))],
        out_specs=pl.BlockSpec((1, tm, d), row),
        out_shape=jax.ShapeDtypeStruct((b, t, d), F32),
        compiler_params=_params("parallel", "parallel"),
        name="merge_out_proj",
    )(x, p['norm_g'], mod, mod, mod, gm, na, yf, yb, prep, rwg, p['rw_gn_g'], p['rw_gn_b'], plb,
      p['w_merge'], p['w_br'], p['w_out'], p['final_g'])


def _layer(x, ctx, layer, mod, p, last):
    b = x.shape[0]
    latent_row = lambda i: i
    ctx_row = lambda i: b

    gm_c, na_c, prep_c, rwg_c, pl_c = _project_mix(ctx, layer, mod, ctx_row, p)
    gm_x, na_x, prep_x, rwg_x, pl_x = _project_mix(x, layer, mod, latent_row, p)

    zero_state = jnp.zeros((b, 2, BR_W // LANES, LANES, LANES), F32)
    yf_c, yb_c, s_ctx = _rwkv_scan(prep_c, zero_state)
    yf_x, yb_x, _ = _rwkv_scan(prep_x, s_ctx)

    x_new = _merge(x, layer, mod, latent_row, gm_x, _natten(na_x, na_c, layer, p['na_bias']),
                   yf_x, yb_x, prep_x, rwg_x, _pool(pl_x, layer, p), p, final=last)
    if last:
        return x_new, ctx
    ctx_new = _merge(ctx, layer, mod, ctx_row, gm_c, _ctx_attention(na_c),
                     yf_c, yb_c, prep_c, rwg_c, _pool(pl_c, layer, p), p, final=False)
    return x_new, ctx_new


def kernel(x, c, ctx, c_ctx, ada_w, ada_b, norm_g, w_in, gm_ln_g, gm_ln_b, gm_ws, gm_bs, na_rpb, rw_mu, rw_w0, rw_w2, rw_a0, rw_a2, rw_kk, rw_ka, rw_rk, rw_gn_g, rw_gn_b, pl_w, pl_scale, w_br, w_out, final_g):
    b, _, d = x.shape
    depth = ada_w.shape[0]
    mod_rows = 16
    cs = jnp.zeros((mod_rows, d), F32).at[:b].set(c).at[b].set(c_ctx)
    mod = _modulation(cs, ada_w, ada_b).reshape(depth, mod_rows, 1, 3 * d)

    vec = lambda a: a.reshape(depth, 1, -1)
    lora_zeros = jnp.zeros((depth, LORA, 2 * BR_W), F32)
    side = lambda w: jnp.concatenate([w[:, 0], w[:, 1]], axis=2)
    lora = jnp.concatenate([jnp.concatenate([side(rw_w2), lora_zeros], axis=2),
                            jnp.concatenate([lora_zeros, side(rw_a2)], axis=2)], axis=1).astype(BF16)
    p = {
        'norm_g': vec(norm_g), 'final_g': final_g.reshape(1, d),
        'w_mix': w_in[:, :, :MIX_COLS].astype(BF16), 'w_merge': w_in[:, :, MIX_COLS:].astype(BF16),
        'w_br': w_br.astype(BF16), 'w_out': w_out.astype(BF16),
        'gm_ln_g': vec(gm_ln_g), 'gm_ln_b': vec(gm_ln_b), 'gm_ws': gm_ws.transpose(0, 2, 1, 3).reshape(depth, CHUNK, N_HEADS * CHUNK).astype(BF16),
        'gm_bs': jnp.repeat(gm_bs.transpose(0, 2, 1), HEAD_DIM, axis=2),
        'na_bias': _natten_bias(na_rpb),
        'rw_mu': vec(rw_mu), 'rw_kk': vec(rw_kk), 'rw_ka': vec(rw_ka), 'rw_rk': vec(rw_rk),
        'rw_w0': rw_w0, 'rw_a0': rw_a0,
        'rw_lora': jnp.concatenate([lora, lora], axis=1),
        'rw_gn_g': vec(rw_gn_g), 'rw_gn_b': vec(rw_gn_b),
        'pl_w': jnp.einsum('lgcd,gh->lgchd', pl_w, jnp.eye(N_HEADS, dtype=F32)).reshape(depth, BR_W, BR_W).astype(BF16),
        'pl_scale': vec(pl_scale),
    }
    for layer in range(depth):
        x, ctx = _layer(x, ctx, layer, mod, p, last=(layer == depth - 1))
    return x
```

```python
import functools

import jax
import jax.numpy as jnp
import numpy as np
from jax import lax
from jax.experimental import pallas as pl
from jax.experimental.pallas import tpu as pltpu

F32 = jnp.float32
BF16 = jnp.bfloat16

N_HEADS = 4
N_BRANCH = 4
HEAD_DIM = 64
BR_W = N_HEADS * HEAD_DIM
LANES = 128
GRID_W = 64
NA_ROWS = 8
NA_COLS = 16
NA_RT = 16
CHUNK = 128
LORA = 64
POOL_WINDOWS = (2, 4, 8, 16)
POOL_HALO = 16
SCAN_L = 64
SCAN_BLK = 16
SCAN_NB = 4
EPS = 1e-6
GN_EPS = 64e-5
NEG = -1e30
VMEM_LIMIT = 56 * 1024 * 1024

SEG_GM, SEG_NA, SEG_RWIN, SEG_RWG, SEG_PL = 3 * BR_W, 4 * BR_W, 3 * BR_W + 2 * LORA, BR_W, 2 * BR_W
MIX_COLS = SEG_GM + SEG_NA + SEG_RWIN + SEG_RWG + SEG_PL


def _params(*sem):
    return pltpu.CompilerParams(dimension_semantics=sem, vmem_limit_bytes=VMEM_LIMIT)


def _sigmoid(x):
    return 1.0 / (1.0 + jnp.exp(-x))


def _sigmoid_t(x):
    return 0.5 * jnp.tanh(0.5 * x) + 0.5


def _silu(x):
    return x * _sigmoid_t(x)


def _gelu(x):
    return 0.5 * x * (1.0 + jnp.tanh(0.7978845608028654 * (x + 0.044715 * x * x * x)))


def _mm(a, b):
    return jnp.dot(a.astype(BF16), b.astype(BF16), preferred_element_type=F32)


def _mm_nt(a, b):
    return lax.dot_general(a.astype(BF16), b.astype(BF16), (((1,), (1,)), ((), ())),
                           preferred_element_type=F32)


def _mm_tn(a, b):
    return lax.dot_general(a.astype(BF16), b.astype(BF16), (((0,), (0,)), ((), ())),
                           preferred_element_type=F32)


def _split(x):
    hi = x.astype(BF16)
    lo = (x - hi.astype(F32)).astype(BF16)
    return hi, lo


def _mm3(a, b):
    ah, al = _split(a)
    bh, bl = _split(b)
    d = functools.partial(jnp.dot, preferred_element_type=F32)
    return d(ah, bh) + d(ah, bl) + d(al, bh)


def _mm_exact_rhs(a, b_bf16, precise=False):
    d = functools.partial(jnp.dot, preferred_element_type=F32)
    if not precise:
        return d(a.astype(BF16), b_bf16)
    a_hi, a_lo = _split(a)
    return d(a_hi, b_bf16) + d(a_lo, b_bf16)


def _group_matrix(value):
    i = lax.broadcasted_iota(jnp.int32, (BR_W, BR_W), 0) // HEAD_DIM
    j = lax.broadcasted_iota(jnp.int32, (BR_W, BR_W), 1) // HEAD_DIM
    return jnp.where(i == j, value, 0.0).astype(BF16)


def _head_of_lane(shape):
    return lax.broadcasted_iota(jnp.int32, shape, len(shape) - 1) // HEAD_DIM


def _norm_mod(x, g, shift, scale):
    ms = jnp.mean(x * x, axis=-1, keepdims=True)
    return (x * lax.rsqrt(ms + EPS) * g) * (1.0 + scale) + shift


def _mod_kernel(cs_ref, w_ref, b_ref, o_ref):
    o_ref[0] = _mm3(_silu(cs_ref[...]), w_ref[0]) + b_ref[0]


def _modulation(cs, ada_w, ada_b):
    depth, d, d3 = ada_w.shape
    rows = cs.shape[0]
    tn = 768
    return pl.pallas_call(
        _mod_kernel,
        grid=(depth, d3 // tn),
        in_specs=[pl.BlockSpec((rows, d), lambda l, j: (0, 0)),
                  pl.BlockSpec((1, d, tn), lambda l, j: (l, 0, j)),
                  pl.BlockSpec((1, 1, tn), lambda l, j: (l, 0, j))],
        out_specs=pl.BlockSpec((1, rows, tn), lambda l, j: (l, 0, j)),
        out_shape=jax.ShapeDtypeStruct((depth, rows, d3), F32),
        compiler_params=_params("arbitrary", "arbitrary"),
        name="adaln_modulation",
    )(cs, ada_w, ada_b.reshape(depth, 1, d3))


def _layer_spec(tail, layer):
    zeros = (0,) * len(tail)
    return pl.BlockSpec((1,) + tuple(tail), lambda i, j: (layer,) + zeros)


def _gmlp_mix(z, lng, lnb, ws_ref, bs, o_ref, row0):
    tm = z.shape[0]
    u = _gelu(z[:, 0:BR_W])
    v = _gelu(z[:, BR_W:2 * BR_W])
    gate = z[:, 2 * BR_W:3 * BR_W]
    avg = _group_matrix(1.0 / HEAD_DIM)
    mu = _mm_exact_rhs(v, avg, precise=True)
    vc = v - mu
    var = _mm_exact_rhs(vc * vc, avg)
    vh = (vc * lax.rsqrt(var + EPS) * lng + lnb).astype(BF16)
    su = u * _silu(gate)
    for c in range(tm // CHUNK):
        rows = slice(c * CHUNK, (c + 1) * CHUNK)
        mixed = jnp.dot(ws_ref[0], _stack_heads(vh[rows]), preferred_element_type=F32)
        o_ref[0, row0 + c * CHUNK:row0 + (c + 1) * CHUNK, :] = (su[rows] * (mixed + bs)).astype(o_ref.dtype)


PREP_R, PREP_V, PREP_A = 0, 1, 2
PREP_DIR = 3
PREP_BONUS = 9
PREP_COLS = 10 * BR_W
DECAY_SCALE = 0.6065306597126334


def _token_shift(z, halo, mu):
    tm = z.shape[0]
    n = tm + 8
    ze = jnp.concatenate([z, halo], axis=0)
    neighbours = pltpu.roll(ze, 1, 0)[0:tm] + pltpu.roll(ze, n - 1, 0)[0:tm]
    return z * (1.0 - mu) + neighbours * (0.5 * mu)


def _rwkv_prepare(zf, kkw, ka, rk, w0, a0, lora_ref, o_ref, rows):
    r = zf[:, 0:BR_W]
    k = zf[:, BR_W:2 * BR_W]
    v = zf[:, 2 * BR_W:3 * BR_W]
    lo = zf[:, 3 * BR_W:3 * BR_W + 2 * LORA]
    ones = _group_matrix(1.0)
    kk = k * kkw
    kk = kk / jnp.maximum(jnp.sqrt(_mm_exact_rhs(kk * kk, ones)), 1e-12)
    o_ref[0, rows, PREP_R * BR_W:(PREP_R + 1) * BR_W] = r
    o_ref[0, rows, PREP_V * BR_W:(PREP_V + 1) * BR_W] = v
    o_ref[0, rows, PREP_A * BR_W:(PREP_A + 1) * BR_W] = -kk
    decay_lane = lax.broadcasted_iota(jnp.int32, lo.shape, 1) < LORA
    t_hi, t_lo = _split(jnp.where(decay_lane, jnp.tanh(lo), lo))
    low_rank = jnp.dot(jnp.concatenate([t_hi, t_lo], axis=1), lora_ref[0], preferred_element_type=F32)
    kd_sum = jnp.zeros_like(k)
    for d in range(2):
        y = w0[d:d + 1, :] + low_rank[:, d * BR_W:(d + 1) * BR_W]
        a = _sigmoid(a0[d:d + 1, :] + low_rank[:, (2 + d) * BR_W:(3 + d) * BR_W])
        kd = k * (1.0 + (a - 1.0) * ka)
        c0 = (PREP_DIR + 3 * d) * BR_W
        o_ref[0, rows, c0:c0 + BR_W] = _sigmoid(y) * (-DECAY_SCALE)
        o_ref[0, rows, c0 + BR_W:c0 + 2 * BR_W] = kd
        o_ref[0, rows, c0 + 2 * BR_W:c0 + 3 * BR_W] = kk * a
        kd_sum = kd_sum + kd
    o_ref[0, rows, PREP_BONUS * BR_W:] = _mm_exact_rhs(r * kd_sum * rk, ones) * v


OFF_GM = 0
OFF_NA = OFF_GM + SEG_GM
OFF_RWIN = OFF_NA + SEG_NA
OFF_RWG = OFF_RWIN + SEG_RWIN
OFF_PL = OFF_RWG + SEG_RWG


def _projmix_kernel(x_ref, xp_ref, xn_ref, ng_ref, sh_ref, sc_ref, w_ref,
                    lng_ref, lnb_ref, ws_ref, bs_ref,
                    mu_ref, kkw_ref, ka_ref, rk_ref, w0_ref, a0_ref, lora_ref,
                    gm_ref, na_ref, prep_ref, rwg_ref, pl_ref):
    j = pl.program_id(1)
    tm = x_ref.shape[1]
    g, sh, sc = ng_ref[0], sh_ref[0, 0], sc_ref[0, 0]
    h = _norm_mod(x_ref[0], g, sh, sc).astype(BF16)
    proj = lambda lhs, off, width: jnp.dot(lhs, w_ref[0, :, off:off + width], preferred_element_type=F32)

    xh = jnp.concatenate([xn_ref[0], xp_ref[0]], axis=0)
    z_rw = proj(jnp.concatenate([h, _norm_mod(xh, g, sh, sc).astype(BF16)], axis=0), OFF_RWIN, SEG_RWIN)
    row = lax.broadcasted_iota(jnp.int32, (8, SEG_RWIN), 0)
    has_prev = jnp.where(j > 0, 1.0, 0.0)
    has_next = jnp.where(j < pl.num_programs(1) - 1, 1.0, 0.0)
    halo = (jnp.where(row == 0, z_rw[tm:tm + 8], 0.0) * has_next
            + jnp.where(row == 7, z_rw[tm + 8:tm + 16], 0.0) * has_prev)
    zf = _token_shift(z_rw[0:tm], halo, mu_ref[0])

    z_gm = proj(h, OFF_GM, SEG_GM)
    _rwkv_prepare(zf, kkw_ref[0], ka_ref[0], rk_ref[0], w0_ref[0], a0_ref[0], lora_ref, prep_ref, slice(None))
    na_ref[0] = proj(h, OFF_NA, SEG_NA).astype(na_ref.dtype)
    rwg_ref[0] = proj(h, OFF_RWG, SEG_RWG).astype(rwg_ref.dtype)
    pl_ref[0] = proj(h, OFF_PL, SEG_PL).astype(pl_ref.dtype)
    _gmlp_mix(z_gm, lng_ref[0], lnb_ref[0], ws_ref, bs_ref[0], gm_ref, 0)


def _project_mix(x, layer, mod, mod_row, p):
    b, t, d = x.shape
    tm = min(t, 512)
    nblk8 = t // 8
    row = lambda i, j: (i, j, 0)
    modspec = lambda c: pl.BlockSpec((1, 1, 1, d), lambda i, j: (layer, mod_row(i), 0, c))
    widths = (BR_W, SEG_NA, PREP_COLS, SEG_RWG, SEG_PL)
    dtypes = (BF16, BF16, F32, BF16, BF16)
    return pl.pallas_call(
        _projmix_kernel,
        grid=(b, t // tm),
        in_specs=[pl.BlockSpec((1, tm, d), row),
                  pl.BlockSpec((1, 8, d), lambda i, j: (i, jnp.maximum(j * (tm // 8) - 1, 0), 0)),
                  pl.BlockSpec((1, 8, d), lambda i, j: (i, jnp.minimum((j + 1) * (tm // 8), nblk8 - 1), 0)),
                  _layer_spec((1, d), layer), modspec(0), modspec(1),
                  _layer_spec((d, MIX_COLS), layer),
                  _layer_spec((1, BR_W), layer), _layer_spec((1, BR_W), layer),
                  _layer_spec((CHUNK, N_HEADS * CHUNK), layer), _layer_spec((CHUNK, BR_W), layer),
                  _layer_spec((1, SEG_RWIN), layer), _layer_spec((1, BR_W), layer),
                  _layer_spec((1, BR_W), layer), _layer_spec((1, BR_W), layer),
                  _layer_spec((2, BR_W), layer), _layer_spec((2, BR_W), layer),
                  _layer_spec((4 * LORA, 4 * BR_W), layer)],
        out_specs=[pl.BlockSpec((1, tm, w), row) for w in widths],
        out_shape=[jax.ShapeDtypeStruct((b, t, w), dt) for w, dt in zip(widths, dtypes)],
        compiler_params=_params("parallel", "parallel"),
        name="norm_in_proj_mix",
    )(x, x, x, p['norm_g'], mod, mod, p['w_mix'],
      p['gm_ln_g'], p['gm_ln_b'], p['gm_ws'], p['gm_bs'],
      p['rw_mu'], p['rw_kk'], p['rw_ka'], p['rw_rk'], p['rw_w0'], p['rw_a0'], p['rw_lora'])


def _pool_kernel(p_ref, g_ref, w_ref, sc_ref, o_ref, ext_ref, *, seq_len):
    j = pl.program_id(1)
    tm = o_ref.shape[1]
    n = tm + 2 * POOL_HALO

    @pl.when(j == 0)
    def _():
        ext_ref[0:POOL_HALO, :] = jnp.zeros((POOL_HALO, BR_W), F32)
        ext_ref[POOL_HALO + seq_len:, :] = jnp.zeros((POOL_HALO, BR_W), F32)
        ext_ref[POOL_HALO:POOL_HALO + seq_len, :] = p_ref[0].astype(F32)

    t0 = pl.multiple_of(j * tm, 8)
    x = ext_ref[pl.ds(t0, n), :]
    s2 = x + pltpu.roll(x, 1, 0)
    s4 = pltpu.roll(s2, 1, 0) + pltpu.roll(s2, n - 1, 0)
    s8 = pltpu.roll(s4, 2, 0) + pltpu.roll(s4, n - 2, 0)
    s16 = pltpu.roll(s8, 4, 0) + pltpu.roll(s8, n - 4, 0)
    mid = slice(POOL_HALO, POOL_HALO + tm)
    grp = _head_of_lane((tm, BR_W))
    ssum = jnp.where(grp == 0, s2[mid], jnp.where(grp == 1, s4[mid], jnp.where(grp == 2, s8[mid], s16[mid])))
    half = jnp.where(grp == 0, 1, jnp.where(grp == 1, 2, jnp.where(grp == 2, 4, 8)))
    t = t0 + lax.broadcasted_iota(jnp.int32, (tm, BR_W), 0)
    cnt = jnp.minimum(t + half, seq_len) - jnp.maximum(t - half, 0)
    d = ssum / cnt.astype(F32) - x[mid]
    y = _mm(d, w_ref[0])
    o_ref[0] = (y * sc_ref[0] * _silu(g_ref[0].astype(F32))).astype(o_ref.dtype)


def _pool(plz, layer, p):
    b, t, _ = plz.shape
    tm = min(t, 512)
    return pl.pallas_call(
        functools.partial(_pool_kernel, seq_len=t),
        grid=(b, t // tm),
        in_specs=[pl.BlockSpec((1, t, BR_W), lambda i, j: (i, 0, 0)),
                  pl.BlockSpec((1, tm, BR_W), lambda i, j: (i, j, 1)),
                  _layer_spec((BR_W, BR_W), layer),
                  _layer_spec((1, BR_W), layer)],
        out_specs=pl.BlockSpec((1, tm, BR_W), lambda i, j: (i, j, 0)),
        out_shape=jax.ShapeDtypeStruct((b, t, BR_W), BF16),
        scratch_shapes=[pltpu.VMEM((t + 2 * POOL_HALO, BR_W), F32)],
        compiler_params=_params("parallel", "arbitrary"),
        name="pool_mixer",
    )(plz, plz, p['pl_w'], p['pl_scale'])


def _stack_heads(q):
    head = _head_of_lane(q.shape)
    return jnp.concatenate([jnp.where(head == h, q, 0.0) for h in range(N_HEADS)], axis=0)


def _unstack_heads(o, n):
    head = _head_of_lane((n, BR_W))
    out = jnp.zeros((n, BR_W), F32)
    for h in range(N_HEADS):
        out = jnp.where(head == h, o[h * n:(h + 1) * n], out)
    return out


def _natten_kernel(q_ref, k_ref, v_ref, g_ref, kc_ref, vc_ref, bias_ref, o_ref, *, rows):
    band = NA_ROWS * GRID_W
    rng = range(NA_RT)
    kc = kc_ref[0]
    vc = vc_ref[0]
    r = [pl.program_id(1) * NA_RT + i for i in rng]
    rs = [jnp.clip(r[i] - NA_ROWS // 2, 0, rows - NA_ROWS) for i in rng]
    start = [pl.multiple_of(rs[i] * GRID_W, GRID_W) for i in rng]
    qs = [_stack_heads(q_ref[0, i * GRID_W:(i + 1) * GRID_W, :] * (HEAD_DIM ** -0.5)) for i in rng]
    kb = [k_ref[0, pl.ds(start[i], band), :] for i in rng]
    vb = [v_ref[0, pl.ds(start[i], band), :] for i in rng]
    s_b = [_mm_nt(qs[i], kb[i]) + bias_ref[0, rs[i] - r[i] + NA_ROWS - 1] for i in rng]
    s_c = [_mm_nt(qs[i], kc) for i in rng]
    m = [jnp.maximum(jnp.max(s_b[i], axis=1, keepdims=True), jnp.max(s_c[i], axis=1, keepdims=True)) for i in rng]
    e_b = [jnp.exp(s_b[i] - m[i]) for i in rng]
    e_c = [jnp.exp(s_c[i] - m[i]) for i in rng]
    den = [jnp.sum(e_b[i], axis=1, keepdims=True) + jnp.sum(e_c[i], axis=1, keepdims=True) for i in rng]
    pv = [(_mm(e_b[i], vb[i]) + _mm(e_c[i], vc)) / den[i] for i in rng]
    for i in rng:
        rows_i = slice(i * GRID_W, (i + 1) * GRID_W)
        gate = _silu(g_ref[0, rows_i, :].astype(F32))
        o_ref[0, rows_i, :] = (_unstack_heads(pv[i], GRID_W) * gate).astype(o_ref.dtype)


def _natten_bias(rpb):
    depth = rpb.shape[0]
    c = np.arange(GRID_W)
    c_start = np.clip(c - NA_COLS // 2, 0, GRID_W - NA_COLS)
    col_ok = (c[None, :] >= c_start[:, None]) & (c[None, :] < c_start[:, None] + NA_COLS)
    dc = np.clip(c[None, :] - c[:, None], -(NA_COLS - 1), NA_COLS - 1) + NA_COLS - 1
    select = (np.arange(2 * NA_COLS - 1)[:, None, None] == dc[None]).astype(np.float32)
    rows = jnp.stack([rpb[:, :, o:o + NA_ROWS] for o in range(NA_ROWS)], axis=1)
    tab = jnp.einsum('lohic,cqk->lohqik', rows, jnp.asarray(select), precision=lax.Precision.HIGHEST)
    tab = jnp.where(jnp.asarray(col_ok)[None, None, None, :, None, :], tab, NEG)
    return tab.reshape(depth, NA_ROWS, N_HEADS * GRID_W, NA_ROWS * GRID_W)


def _natten(na, na_ctx, layer, bias):
    b, t, _ = na.shape
    lc = na_ctx.shape[1]
    rows = t // GRID_W
    band = NA_ROWS * GRID_W
    tq = NA_RT * GRID_W
    qrow = lambda c: (lambda i, r: (i, r, c))
    full = lambda c: (lambda i, r: (i, 0, c))
    return pl.pallas_call(
        functools.partial(_natten_kernel, rows=rows),
        grid=(b, rows // NA_RT),
        in_specs=[pl.BlockSpec((1, tq, BR_W), qrow(0)),
                  pl.BlockSpec((1, t, BR_W), full(1)),
                  pl.BlockSpec((1, t, BR_W), full(2)),
                  pl.BlockSpec((1, tq, BR_W), qrow(3)),
                  pl.BlockSpec((1, lc, BR_W), full(1)),
                  pl.BlockSpec((1, lc, BR_W), full(2)),
                  _layer_spec((NA_ROWS, N_HEADS * GRID_W, band), layer)],
        out_specs=pl.BlockSpec((1, tq, BR_W), qrow(0)),
        out_shape=jax.ShapeDtypeStruct((b, t, BR_W), BF16),
        compiler_params=_params("parallel", "arbitrary"),
        name="natten_mixer",
    )(na, na, na, na, na_ctx, na_ctx, bias)


def _ctx_attn_kernel(q_ref, k_ref, v_ref, g_ref, o_ref):
    n = q_ref.shape[1]
    qs = _stack_heads(q_ref[0] * (HEAD_DIM ** -0.5))
    s = _mm_nt(qs, k_ref[0])
    e = jnp.exp(s - jnp.max(s, axis=1, keepdims=True))
    pv = _mm(e, v_ref[0]) / jnp.sum(e, axis=1, keepdims=True)
    o_ref[0] = (_unstack_heads(pv, n) * _silu(g_ref[0].astype(F32))).astype(o_ref.dtype)


def _ctx_attention(na_ctx):
    b, lc, _ = na_ctx.shape
    tq = 64
    qrow = lambda c: (lambda i, j: (i, j, c))
    full = lambda c: (lambda i, j: (i, 0, c))
    return pl.pallas_call(
        _ctx_attn_kernel,
        grid=(b, lc // tq),
        in_specs=[pl.BlockSpec((1, tq, BR_W), qrow(0)),
                  pl.BlockSpec((1, lc, BR_W), full(1)),
                  pl.BlockSpec((1, lc, BR_W), full(2)),
                  pl.BlockSpec((1, tq, BR_W), qrow(3))],
        out_specs=pl.BlockSpec((1, tq, BR_W), qrow(0)),
        out_shape=jax.ShapeDtypeStruct((b, lc, BR_W), BF16),
        compiler_params=_params("parallel", "arbitrary"),
        name="ctx_attention",
    )(na_ctx, na_ctx, na_ctx, na_ctx)


def _stack_pair(x):
    head = _head_of_lane(x.shape)
    return jnp.concatenate([jnp.where(head == 0, x, 0.0), jnp.where(head == 1, x, 0.0)], axis=0)


def _scan_chunks(chains):
    n = chains[0][0].shape[0]
    m = 2 * n
    nc = len(chains)
    rng = range(nc)
    rev = [c[7] for c in chains]
    row_n = lax.broadcasted_iota(jnp.int32, (n, LANES), 0)
    i2 = lax.broadcasted_iota(jnp.int32, (m, 2 * m), 0)
    j2 = lax.broadcasted_iota(jnp.int32, (m, 2 * m), 1) % n
    t2 = i2 % n
    same = jnp.where(i2 < n, 0, 1)
    keep = {False: j2 < t2 + same, True: j2 > t2 - same}
    ri = lax.broadcasted_iota(jnp.int32, (LANES, LANES), 0)
    ci = lax.broadcasted_iota(jnp.int32, (LANES, LANES), 1)
    eye = ri == ci
    same_head = (ri // HEAD_DIM) == (ci // HEAD_DIM)
    d = functools.partial(jnp.dot, preferred_element_type=F32)

    def running_sum(x, reverse):
        s = 1
        while s < n:
            if reverse:
                x = x + jnp.where(row_n < n - s, pltpu.roll(x, n - s, 0), 0.0)
            else:
                x = x + jnp.where(row_n >= s, pltpu.roll(x, s, 0), 0.0)
            s *= 2
        return x

    cum = [running_sum(chains[c][3], rev[c]) for c in rng]
    tot = [jnp.sum(chains[c][3], axis=0, keepdims=True) for c in rng]
    e_neg = [jnp.exp(-cum[c]) for c in rng]
    ar = [jnp.concatenate([chains[c][2] * jnp.exp(cum[c] - chains[c][3]),
                           chains[c][0] * jnp.exp(cum[c])], axis=0).astype(BF16) for c in rng]
    bk_st = [jnp.concatenate([_stack_pair((chains[c][5] * e_neg[c]).astype(BF16)),
                              _stack_pair((chains[c][4] * e_neg[c]).astype(BF16))], axis=0) for c in rng]
    v_bf = [chains[c][1].astype(BF16) for c in rng]

    g = [jnp.where(keep[rev[c]], _mm_nt(ar[c], bk_st[c]), 0.0) for c in rng]
    lhs = [jnp.concatenate([ar[c], g[c][:, m:].astype(BF16)], axis=1) for c in rng]
    rhs = [jnp.concatenate([chains[c][6].astype(BF16), _stack_pair(v_bf[c])], axis=0) for c in rng]
    xy = [d(lhs[c], rhs[c]) for c in rng]

    nblk = n // SCAN_BLK
    row_t = lax.broadcasted_iota(jnp.int32, (n, LANES), 0)
    lane_s = lax.broadcasted_iota(jnp.int32, (n, LANES), 1) % n
    same_blk = row_t // SCAN_BLK == lane_s // SCAN_BLK
    p = [jnp.where(same_blk, g[c][:n, :m], 0.0) for c in rng]
    off = [jnp.where(same_blk, 0.0, g[c][:n, :m]).astype(BF16) for c in rng]
    y0 = [xy[c][:n] for c in rng]
    t_d = [jnp.where(row_t == lane_s, 1.0, 0.0) for c in rng]
    step = 1
    while step < SCAN_BLK:
        step *= 2
        last = step >= SCAN_BLK
        for c in rng:
            p_bf = p[c].astype(BF16)
            y_hi, y_lo = _split(y0[c])
            parts = [_stack_pair(y_hi), _stack_pair(y_lo), _stack_pair(t_d[c].astype(BF16))]
            if not last:
                parts.append(_stack_pair(p_bf))
            acc = d(p_bf, jnp.concatenate(parts, axis=1))
            y0[c] = y0[c] + (acc[:, :LANES] + acc[:, LANES:2 * LANES])
            t_d[c] = t_d[c] + acc[:, 2 * LANES:3 * LANES]
            p[c] = None if last else acc[:, 3 * LANES:]

    u = list(y0)
    for k in range(1, nblk):
        for c in rng:
            blk = nblk - 1 - k if rev[c] else k
            rows = slice(blk * SCAN_BLK, (blk + 1) * SCAN_BLK)
            u_hi, u_lo = _split(u[c])
            w = d(off[c][rows], jnp.concatenate([_stack_pair(u_hi), _stack_pair(u_lo)], axis=1))
            w = w[:, :LANES] + w[:, LANES:]
            w_hi, w_lo = _split(jnp.concatenate([w] * nblk, axis=0))
            v = d(t_d[c][rows].astype(BF16), jnp.concatenate([_stack_pair(w_hi), _stack_pair(w_lo)], axis=1))
            pieces = [u[c][:blk * SCAN_BLK]] if blk > 0 else []
            pieces.append(y0[c][rows] + (v[:, :LANES] + v[:, LANES:]))
            if blk < nblk - 1:
                pieces.append(u[c][(blk + 1) * SCAN_BLK:])
            u[c] = jnp.concatenate(pieces, axis=0)

    outs = []
    for c in rng:
        u_bf = u[c].astype(BF16)
        y = xy[c][n:] + d(g[c][n:, :m].astype(BF16), _stack_pair(u_bf))
        e_rest = jnp.exp(tot[c] - cum[c])
        bk_rest = jnp.concatenate([chains[c][5] * e_rest, chains[c][4] * e_rest], axis=0)
        decay = jnp.sum(jnp.where(eye, jnp.exp(tot[c]), 0.0), axis=1, keepdims=True)
        upd = _mm_tn(bk_rest, jnp.concatenate([u_bf, v_bf[c]], axis=0))
        outs.append((y, chains[c][6] * decay + jnp.where(same_head, upd, 0.0)))
    return outs


def _scan_kernel(sf_ref, df_ref, sb_ref, db_ref, s0_ref, yf_ref, yb_ref, sfin_ref, state_ref):
    j = pl.program_id(1)

    @pl.when(j == 0)
    def _():
        state_ref[...] = s0_ref[...]

    nb = sf_ref.shape[0]
    npair = BR_W // LANES
    chains = []
    for i in range(nb):
        for dr, (s_ref, d_ref) in enumerate(((sf_ref, df_ref), (sb_ref, db_ref))):
            for p in range(npair):
                col = lambda blk: slice(blk * BR_W + p * LANES, blk * BR_W + (p + 1) * LANES)
                chains.append((s_ref[i, :, col(PREP_R)], s_ref[i, :, col(PREP_V)], s_ref[i, :, col(PREP_A)],
                               d_ref[i, :, col(0)], d_ref[i, :, col(1)], d_ref[i, :, col(2)],
                               state_ref[i, dr, p], dr == 1))
    outs = iter(_scan_chunks(chains))
    for i in range(nb):
        for dr, y_ref in enumerate((yf_ref, yb_ref)):
            for p in range(npair):
                y, new_state = next(outs)
                y_ref[i, :, p * LANES:(p + 1) * LANES] = y
                state_ref[i, dr, p] = new_state

    @pl.when(j == pl.num_programs(1) - 1)
    def _():
        sfin_ref[...] = state_ref[...]


def _rwkv_scan(prep, s0):
    b, t, _ = prep.shape
    n = t // SCAN_L
    npair = BR_W // LANES
    blk3 = 3 * BR_W
    fwd = lambda c: (lambda i, j: (i, j, c))
    bwd = lambda c: (lambda i, j: (i, n - 1 - j, c))
    st = lambda i, j: (i, 0, 0, 0, 0)
    nb = SCAN_NB if b % SCAN_NB == 0 else 1
    return pl.pallas_call(
        _scan_kernel,
        grid=(b // nb, n),
        in_specs=[pl.BlockSpec((nb, SCAN_L, blk3), fwd(0)),
                  pl.BlockSpec((nb, SCAN_L, blk3), fwd(1)),
                  pl.BlockSpec((nb, SCAN_L, blk3), bwd(0)),
                  pl.BlockSpec((nb, SCAN_L, blk3), bwd(2)),
                  pl.BlockSpec((nb, 2, npair, LANES, LANES), st)],
        out_specs=[pl.BlockSpec((nb, SCAN_L, BR_W), fwd(0)),
                   pl.BlockSpec((nb, SCAN_L, BR_W), bwd(0)),
                   pl.BlockSpec((nb, 2, npair, LANES, LANES), st)],
        out_shape=[jax.ShapeDtypeStruct((b, t, BR_W), F32),
                   jax.ShapeDtypeStruct((b, t, BR_W), F32),
                   jax.ShapeDtypeStruct((b, 2, npair, LANES, LANES), F32)],
        scratch_shapes=[pltpu.VMEM((nb, 2, npair, LANES, LANES), F32)],
        compiler_params=_params("parallel", "arbitrary"),
        name="rwkv_scan",
    )(prep, prep, prep, prep, s0)


def _rwkv_finish(y_sum, bonus, gate, gn_g, gn_b):
    avg = _group_matrix(1.0 / HEAD_DIM)
    mu = _mm_exact_rhs(y_sum, avg, precise=True)
    oc = y_sum - mu
    var = _mm_exact_rhs(oc * oc, avg)
    o = oc * lax.rsqrt(var + GN_EPS) * gn_g + gn_b + bonus
    return o * _silu(gate)


def _merge_kernel(x_ref, g_ref, sh_ref, sc_ref, gt_ref, gm_ref, na_ref, yf_ref, yb_ref, bonus_ref, rwg_ref,
                  gng_ref, gnb_ref, pl_ref, wm_ref, wbr_ref, wout_ref, fin_ref, o_ref, *, final):
    x = x_ref[0]
    d = x.shape[-1]
    h = _norm_mod(x, g_ref[0], sh_ref[0, 0], sc_ref[0, 0]).astype(BF16)
    rw = _rwkv_finish(yf_ref[0] + yb_ref[0], bonus_ref[0], rwg_ref[0].astype(F32), gng_ref[0], gnb_ref[0])
    y = None
    for i, br in enumerate((gm_ref[0], na_ref[0], rw.astype(BF16), pl_ref[0])):
        logits = jnp.dot(h, wm_ref[0, :, i * d:(i + 1) * d], preferred_element_type=F32)
        term = _sigmoid_t(logits) * jnp.dot(br, wbr_ref[0, i], preferred_element_type=F32)
        y = term if y is None else y + term
    out = x + gt_ref[0, 0] * jnp.dot(y.astype(BF16), wout_ref[0], preferred_element_type=F32)
    if final:
        ms = jnp.mean(out * out, axis=-1, keepdims=True)
        out = out * lax.rsqrt(ms + EPS) * fin_ref[...]
    o_ref[0] = out


def _merge(x, layer, mod, mod_row, gm, na, yf, yb, prep, rwg, plb, p, final):
    b, t, d = x.shape
    tm = min(t, 512)
    row = lambda i, j: (i, j, 0)
    modspec = lambda c: pl.BlockSpec((1, 1, 1, d), lambda i, j: (layer, mod_row(i), 0, c))
    br = pl.BlockSpec((1, tm, BR_W), row)
    return pl.pallas_call(
        functools.partial(_merge_kernel, final=final),
        grid=(b, t // tm),
        in_specs=[pl.BlockSpec((1, tm, d), row), _layer_spec((1, d), layer), modspec(0), modspec(1), modspec(2),
                  br, br, br, br, pl.BlockSpec((1, tm, BR_W), lambda i, j: (i, j, PREP_BONUS)), br,
                  _layer_spec((1, BR_W), layer), _layer_spec((1, BR_W), layer), br,
                  _layer_spec((d, N_BRANCH * d), layer), _layer_spec((N_BRANCH, BR_W, d), layer),
                  _layer_spec((d, d), layer), pl.BlockSpec((1, d), lambda i, j: (0, 0))],
        out_specs=pl.BlockSpec((1, tm, d), row),
        out_shape=jax.ShapeDtypeStruct((b, t, d), F32),
        compiler_params=_params("parallel", "parallel"),
        name="merge_out_proj",
    )(x, p['norm_g'], mod, mod, mod, gm, na, yf, yb, prep, rwg, p['rw_gn_g'], p['rw_gn_b'], plb,
      p['w_merge'], p['w_br'], p['w_out'], p['final_g'])


def _layer(x, ctx, layer, mod, p, last):
    b = x.shape[0]
    latent_row = lambda i: i
    ctx_row = lambda i: b

    gm_c, na_c, prep_c, rwg_c, pl_c = _project_mix(ctx, layer, mod, ctx_row, p)
    gm_x, na_x, prep_x, rwg_x, pl_x = _project_mix(x, layer, mod, latent_row, p)

    zero_state = jnp.zeros((b, 2, BR_W // LANES, LANES, LANES), F32)
    yf_c, yb_c, s_ctx = _rwkv_scan(prep_c, zero_state)
    yf_x, yb_x, _ = _rwkv_scan(prep_x, s_ctx)

    x_new = _merge(x, layer, mod, latent_row, gm_x, _natten(na_x, na_c, layer, p['na_bias']),
                   yf_x, yb_x, prep_x, rwg_x, _pool(pl_x, layer, p), p, final=last)
    if last:
        return x_new, ctx
    ctx_new = _merge(ctx, layer, mod, ctx_row, gm_c, _ctx_attention(na_c),
                     yf_c, yb_c, prep_c, rwg_c, _pool(pl_c, layer, p), p, final=False)
    return x_new, ctx_new


def kernel(x, c, ctx, c_ctx, ada_w, ada_b, norm_g, w_in, gm_ln_g, gm_ln_b, gm_ws, gm_bs, na_rpb, rw_mu, rw_w0, rw_w2, rw_a0, rw_a2, rw_kk, rw_ka, rw_rk, rw_gn_g, rw_gn_b, pl_w, pl_scale, w_br, w_out, final_g):
    b, _, d = x.shape
    depth = ada_w.shape[0]
    mod_rows = 16
    cs = jnp.zeros((mod_rows, d), F32).at[:b].set(c).at[b].set(c_ctx)
    mod = _modulation(cs, ada_w, ada_b).reshape(depth, mod_rows, 1, 3 * d)

    vec = lambda a: a.reshape(depth, 1, -1)
    lora_zeros = jnp.zeros((depth, LORA, 2 * BR_W), F32)
    side = lambda w: jnp.concatenate([w[:, 0], w[:, 1]], axis=2)
    lora = jnp.concatenate([jnp.concatenate([side(rw_w2), lora_zeros], axis=2),
                            jnp.concatenate([lora_zeros, side(rw_a2)], axis=2)], axis=1).astype(BF16)
    p = {
        'norm_g': vec(norm_g), 'final_g': final_g.reshape(1, d),
        'w_mix': w_in[:, :, :MIX_COLS].astype(BF16), 'w_merge': w_in[:, :, MIX_COLS:].astype(BF16),
        'w_br': w_br.astype(BF16), 'w_out': w_out.astype(BF16),
        'gm_ln_g': vec(gm_ln_g), 'gm_ln_b': vec(gm_ln_b), 'gm_ws': gm_ws.transpose(0, 2, 1, 3).reshape(depth, CHUNK, N_HEADS * CHUNK).astype(BF16),
        'gm_bs': jnp.repeat(gm_bs.transpose(0, 2, 1), HEAD_DIM, axis=2),
        'na_bias': _natten_bias(na_rpb),
        'rw_mu': vec(rw_mu), 'rw_kk': vec(rw_kk), 'rw_ka': vec(rw_ka), 'rw_rk': vec(rw_rk),
        'rw_w0': rw_w0, 'rw_a0': rw_a0,
        'rw_lora': jnp.concatenate([lora, lora], axis=1),
        'rw_gn_g': vec(rw_gn_g), 'rw_gn_b': vec(rw_gn_b),
        'pl_w': jnp.einsum('lgcd,gh->lgchd', pl_w, jnp.eye(N_HEADS, dtype=F32)).reshape(depth, BR_W, BR_W).astype(BF16),
        'pl_scale': vec(pl_scale),
    }
    for layer in range(depth):
        x, ctx = _layer(x, ctx, layer, mod, p, last=(layer == depth - 1))
    return x
```

```python
import functools

import jax
import jax.numpy as jnp
import numpy as np
from jax import lax
from jax.experimental import pallas as pl
from jax.experimental.pallas import tpu as pltpu

F32 = jnp.float32
BF16 = jnp.bfloat16

N_HEADS = 4
N_BRANCH = 4
HEAD_DIM = 64
BR_W = N_HEADS * HEAD_DIM
LANES = 128
GRID_W = 64
NA_ROWS = 8
NA_COLS = 16
NA_RT = 16
CHUNK = 128
LORA = 64
POOL_WINDOWS = (2, 4, 8, 16)
POOL_HALO = 16
SCAN_L = 64
SCAN_BLK = 16
SCAN_NB = 8
EPS = 1e-6
GN_EPS = 64e-5
NEG = -1e30
VMEM_LIMIT = 56 * 1024 * 1024

SEG_GM, SEG_NA, SEG_RWIN, SEG_RWG, SEG_PL = 3 * BR_W, 4 * BR_W, 3 * BR_W + 2 * LORA, BR_W, 2 * BR_W
MIX_COLS = SEG_GM + SEG_NA + SEG_RWIN + SEG_RWG + SEG_PL


def _params(*sem):
    return pltpu.CompilerParams(dimension_semantics=sem, vmem_limit_bytes=VMEM_LIMIT)


def _sigmoid(x):
    return 1.0 / (1.0 + jnp.exp(-x))


def _sigmoid_t(x):
    return 0.5 * jnp.tanh(0.5 * x) + 0.5


def _silu(x):
    return x * _sigmoid_t(x)


def _gelu(x):
    return 0.5 * x * (1.0 + jnp.tanh(0.7978845608028654 * (x + 0.044715 * x * x * x)))


def _mm(a, b):
    return jnp.dot(a.astype(BF16), b.astype(BF16), preferred_element_type=F32)


def _mm_nt(a, b):
    return lax.dot_general(a.astype(BF16), b.astype(BF16), (((1,), (1,)), ((), ())),
                           preferred_element_type=F32)


def _mm_tn(a, b):
    return lax.dot_general(a.astype(BF16), b.astype(BF16), (((0,), (0,)), ((), ())),
                           preferred_element_type=F32)


def _split(x):
    hi = x.astype(BF16)
    lo = (x - hi.astype(F32)).astype(BF16)
    return hi, lo


def _mm3(a, b):
    ah, al = _split(a)
    bh, bl = _split(b)
    d = functools.partial(jnp.dot, preferred_element_type=F32)
    return d(ah, bh) + d(ah, bl) + d(al, bh)


def _mm_exact_rhs(a, b_bf16, precise=False):
    d = functools.partial(jnp.dot, preferred_element_type=F32)
    if not precise:
        return d(a.astype(BF16), b_bf16)
    a_hi, a_lo = _split(a)
    return d(a_hi, b_bf16) + d(a_lo, b_bf16)


def _group_matrix(value):
    i = lax.broadcasted_iota(jnp.int32, (BR_W, BR_W), 0) // HEAD_DIM
    j = lax.broadcasted_iota(jnp.int32, (BR_W, BR_W), 1) // HEAD_DIM
    return jnp.where(i == j, value, 0.0).astype(BF16)


def _head_of_lane(shape):
    return lax.broadcasted_iota(jnp.int32, shape, len(shape) - 1) // HEAD_DIM


def _norm_mod(x, g, shift, scale):
    ms = jnp.mean(x * x, axis=-1, keepdims=True)
    return (x * lax.rsqrt(ms + EPS) * g) * (1.0 + scale) + shift


def _mod_kernel(cs_ref, w_ref, b_ref, o_ref):
    o_ref[0] = _mm3(_silu(cs_ref[...]), w_ref[0]) + b_ref[0]


def _modulation(cs, ada_w, ada_b):
    depth, d, d3 = ada_w.shape
    rows = cs.shape[0]
    tn = 768
    return pl.pallas_call(
        _mod_kernel,
        grid=(depth, d3 // tn),
        in_specs=[pl.BlockSpec((rows, d), lambda l, j: (0, 0)),
                  pl.BlockSpec((1, d, tn), lambda l, j: (l, 0, j)),
                  pl.BlockSpec((1, 1, tn), lambda l, j: (l, 0, j))],
        out_specs=pl.BlockSpec((1, rows, tn), lambda l, j: (l, 0, j)),
        out_shape=jax.ShapeDtypeStruct((depth, rows, d3), F32),
        compiler_params=_params("arbitrary", "arbitrary"),
        name="adaln_modulation",
    )(cs, ada_w, ada_b.reshape(depth, 1, d3))


def _layer_spec(tail, layer):
    zeros = (0,) * len(tail)
    return pl.BlockSpec((1,) + tuple(tail), lambda i, j: (layer,) + zeros, pipeline_mode=pl.Buffered(1))


def _gmlp_mix(z, lng, lnb, ws_ref, bs, o_ref, row0):
    tm = z.shape[0]
    u = _gelu(z[:, 0:BR_W])
    v = _gelu(z[:, BR_W:2 * BR_W])
    gate = z[:, 2 * BR_W:3 * BR_W]
    avg = _group_matrix(1.0 / HEAD_DIM)
    mu = _mm_exact_rhs(v, avg, precise=True)
    vc = v - mu
    var = _mm_exact_rhs(vc * vc, avg)
    vh = (vc * lax.rsqrt(var + EPS) * lng + lnb).astype(BF16)
    su = u * _silu(gate)
    for c in range(tm // CHUNK):
        rows = slice(c * CHUNK, (c + 1) * CHUNK)
        mixed = jnp.dot(ws_ref[0], _stack_heads(vh[rows]), preferred_element_type=F32)
        o_ref[0, row0 + c * CHUNK:row0 + (c + 1) * CHUNK, :] = (su[rows] * (mixed + bs)).astype(o_ref.dtype)


PREP_R, PREP_V, PREP_A = 0, 1, 2
PREP_DIR = 3
PREP_BONUS = 9
PREP_COLS = 10 * BR_W
DECAY_SCALE = 0.6065306597126334


def _token_shift(z, halo, mu):
    tm = z.shape[0]
    n = tm + 8
    ze = jnp.concatenate([z, halo], axis=0)
    neighbours = pltpu.roll(ze, 1, 0)[0:tm] + pltpu.roll(ze, n - 1, 0)[0:tm]
    return z * (1.0 - mu) + neighbours * (0.5 * mu)


def _rwkv_prepare(zf, kkw, ka, rk, w0, a0, lora_ref, o_ref, rows):
    r = zf[:, 0:BR_W]
    k = zf[:, BR_W:2 * BR_W]
    v = zf[:, 2 * BR_W:3 * BR_W]
    lo = zf[:, 3 * BR_W:3 * BR_W + 2 * LORA]
    ones = _group_matrix(1.0)
    kk = k * kkw
    kk = kk / jnp.maximum(jnp.sqrt(_mm_exact_rhs(kk * kk, ones)), 1e-12)
    o_ref[0, rows, PREP_R * BR_W:(PREP_R + 1) * BR_W] = r
    o_ref[0, rows, PREP_V * BR_W:(PREP_V + 1) * BR_W] = v
    o_ref[0, rows, PREP_A * BR_W:(PREP_A + 1) * BR_W] = -kk
    decay_lane = lax.broadcasted_iota(jnp.int32, lo.shape, 1) < LORA
    t_hi, t_lo = _split(jnp.where(decay_lane, jnp.tanh(lo), lo))
    low_rank = jnp.dot(jnp.concatenate([t_hi, t_lo], axis=1), lora_ref[0], preferred_element_type=F32)
    kd_sum = jnp.zeros_like(k)
    for d in range(2):
        y = w0[d:d + 1, :] + low_rank[:, d * BR_W:(d + 1) * BR_W]
        a = _sigmoid(a0[d:d + 1, :] + low_rank[:, (2 + d) * BR_W:(3 + d) * BR_W])
        kd = k * (1.0 + (a - 1.0) * ka)
        c0 = (PREP_DIR + 3 * d) * BR_W
        o_ref[0, rows, c0:c0 + BR_W] = _sigmoid(y) * (-DECAY_SCALE)
        o_ref[0, rows, c0 + BR_W:c0 + 2 * BR_W] = kd
        o_ref[0, rows, c0 + 2 * BR_W:c0 + 3 * BR_W] = kk * a
        kd_sum = kd_sum + kd
    o_ref[0, rows, PREP_BONUS * BR_W:] = _mm_exact_rhs(r * kd_sum * rk, ones) * v


OFF_GM = 0
OFF_NA = OFF_GM + SEG_GM
OFF_RWIN = OFF_NA + SEG_NA
OFF_RWG = OFF_RWIN + SEG_RWIN
OFF_PL = OFF_RWG + SEG_RWG


def _projmix_kernel(x_ref, xp_ref, xn_ref, ng_ref, sh_ref, sc_ref, w_ref,
                    lng_ref, lnb_ref, ws_ref, bs_ref,
                    mu_ref, kkw_ref, ka_ref, rk_ref, w0_ref, a0_ref, lora_ref,
                    gm_ref, na_ref, prep_ref, rwg_ref, pl_ref):
    j = pl.program_id(1)
    tm = x_ref.shape[1]
    g, sh, sc = ng_ref[0], sh_ref[0, 0], sc_ref[0, 0]
    h = _norm_mod(x_ref[0], g, sh, sc).astype(BF16)
    proj = lambda lhs, off, width: jnp.dot(lhs, w_ref[0, :, off:off + width], preferred_element_type=F32)

    xh = jnp.concatenate([xn_ref[0], xp_ref[0]], axis=0)
    z_rw = proj(jnp.concatenate([h, _norm_mod(xh, g, sh, sc).astype(BF16)], axis=0), OFF_RWIN, SEG_RWIN)
    row = lax.broadcasted_iota(jnp.int32, (8, SEG_RWIN), 0)
    has_prev = jnp.where(j > 0, 1.0, 0.0)
    has_next = jnp.where(j < pl.num_programs(1) - 1, 1.0, 0.0)
    halo = (jnp.where(row == 0, z_rw[tm:tm + 8], 0.0) * has_next
            + jnp.where(row == 7, z_rw[tm + 8:tm + 16], 0.0) * has_prev)
    zf = _token_shift(z_rw[0:tm], halo, mu_ref[0])

    z_gm = proj(h, OFF_GM, SEG_GM)
    _rwkv_prepare(zf, kkw_ref[0], ka_ref[0], rk_ref[0], w0_ref[0], a0_ref[0], lora_ref, prep_ref, slice(None))
    na_ref[0] = proj(h, OFF_NA, SEG_NA).astype(na_ref.dtype)
    rwg_ref[0] = proj(h, OFF_RWG, SEG_RWG).astype(rwg_ref.dtype)
    pl_ref[0] = proj(h, OFF_PL, SEG_PL).astype(pl_ref.dtype)
    _gmlp_mix(z_gm, lng_ref[0], lnb_ref[0], ws_ref, bs_ref[0], gm_ref, 0)


def _project_mix(x, layer, mod, mod_row, p):
    b, t, d = x.shape
    tm = min(t, 512)
    nblk8 = t // 8
    row = lambda i, j: (i, j, 0)
    modspec = lambda c: pl.BlockSpec((1, 1, 1, d), lambda i, j: (layer, mod_row(i), 0, c))
    widths = (BR_W, SEG_NA, PREP_COLS, SEG_RWG, SEG_PL)
    dtypes = (BF16, BF16, F32, BF16, BF16)
    return pl.pallas_call(
        _projmix_kernel,
        grid=(b, t // tm),
        in_specs=[pl.BlockSpec((1, tm, d), row),
                  pl.BlockSpec((1, 8, d), lambda i, j: (i, jnp.maximum(j * (tm // 8) - 1, 0), 0)),
                  pl.BlockSpec((1, 8, d), lambda i, j: (i, jnp.minimum((j + 1) * (tm // 8), nblk8 - 1), 0)),
                  _layer_spec((1, d), layer), modspec(0), modspec(1),
                  _layer_spec((d, MIX_COLS), layer),
                  _layer_spec((1, BR_W), layer), _layer_spec((1, BR_W), layer),
                  _layer_spec((CHUNK, N_HEADS * CHUNK), layer), _layer_spec((CHUNK, BR_W), layer),
                  _layer_spec((1, SEG_RWIN), layer), _layer_spec((1, BR_W), layer),
                  _layer_spec((1, BR_W), layer), _layer_spec((1, BR_W), layer),
                  _layer_spec((2, BR_W), layer), _layer_spec((2, BR_W), layer),
                  _layer_spec((4 * LORA, 4 * BR_W), layer)],
        out_specs=[pl.BlockSpec((1, tm, w), row) for w in widths],
        out_shape=[jax.ShapeDtypeStruct((b, t, w), dt) for w, dt in zip(widths, dtypes)],
        compiler_params=_params("parallel", "parallel"),
        name="norm_in_proj_mix",
    )(x, x, x, p['norm_g'], mod, mod, p['w_mix'],
      p['gm_ln_g'], p['gm_ln_b'], p['gm_ws'], p['gm_bs'],
      p['rw_mu'], p['rw_kk'], p['rw_ka'], p['rw_rk'], p['rw_w0'], p['rw_a0'], p['rw_lora'])


def _pool_kernel(p_ref, g_ref, w_ref, sc_ref, o_ref, ext_ref, *, seq_len):
    j = pl.program_id(1)
    tm = o_ref.shape[1]
    n = tm + 2 * POOL_HALO

    @pl.when(j == 0)
    def _():
        ext_ref[0:POOL_HALO, :] = jnp.zeros((POOL_HALO, BR_W), F32)
        ext_ref[POOL_HALO + seq_len:, :] = jnp.zeros((POOL_HALO, BR_W), F32)
        ext_ref[POOL_HALO:POOL_HALO + seq_len, :] = p_ref[0].astype(F32)

    t0 = pl.multiple_of(j * tm, 8)
    x = ext_ref[pl.ds(t0, n), :]
    s2 = x + pltpu.roll(x, 1, 0)
    s4 = pltpu.roll(s2, 1, 0) + pltpu.roll(s2, n - 1, 0)
    s8 = pltpu.roll(s4, 2, 0) + pltpu.roll(s4, n - 2, 0)
    s16 = pltpu.roll(s8, 4, 0) + pltpu.roll(s8, n - 4, 0)
    mid = slice(POOL_HALO, POOL_HALO + tm)
    grp = _head_of_lane((tm, BR_W))
    ssum = jnp.where(grp == 0, s2[mid], jnp.where(grp == 1, s4[mid], jnp.where(grp == 2, s8[mid], s16[mid])))
    half = jnp.where(grp == 0, 1, jnp.where(grp == 1, 2, jnp.where(grp == 2, 4, 8)))
    t = t0 + lax.broadcasted_iota(jnp.int32, (tm, BR_W), 0)
    cnt = jnp.minimum(t + half, seq_len) - jnp.maximum(t - half, 0)
    d = ssum / cnt.astype(F32) - x[mid]
    y = _mm(d, w_ref[0])
    o_ref[0] = (y * sc_ref[0] * _silu(g_ref[0].astype(F32))).astype(o_ref.dtype)


def _pool(plz, layer, p):
    b, t, _ = plz.shape
    tm = min(t, 512)
    return pl.pallas_call(
        functools.partial(_pool_kernel, seq_len=t),
        grid=(b, t // tm),
        in_specs=[pl.BlockSpec((1, t, BR_W), lambda i, j: (i, 0, 0)),
                  pl.BlockSpec((1, tm, BR_W), lambda i, j: (i, j, 1)),
                  _layer_spec((BR_W, BR_W), layer),
                  _layer_spec((1, BR_W), layer)],
        out_specs=pl.BlockSpec((1, tm, BR_W), lambda i, j: (i, j, 0)),
        out_shape=jax.ShapeDtypeStruct((b, t, BR_W), BF16),
        scratch_shapes=[pltpu.VMEM((t + 2 * POOL_HALO, BR_W), F32)],
        compiler_params=_params("parallel", "arbitrary"),
        name="pool_mixer",
    )(plz, plz, p['pl_w'], p['pl_scale'])


def _stack_heads(q):
    head = _head_of_lane(q.shape)
    return jnp.concatenate([jnp.where(head == h, q, 0.0) for h in range(N_HEADS)], axis=0)


def _unstack_heads(o, n):
    head = _head_of_lane((n, BR_W))
    out = jnp.zeros((n, BR_W), F32)
    for h in range(N_HEADS):
        out = jnp.where(head == h, o[h * n:(h + 1) * n], out)
    return out


def _natten_kernel(q_ref, k_ref, v_ref, g_ref, kc_ref, vc_ref, bias_ref, o_ref, *, rows):
    band = NA_ROWS * GRID_W
    rng = range(NA_RT)
    kc = kc_ref[0]
    vc = vc_ref[0]
    r = [pl.program_id(1) * NA_RT + i for i in rng]
    rs = [jnp.clip(r[i] - NA_ROWS // 2, 0, rows - NA_ROWS) for i in rng]
    start = [pl.multiple_of(rs[i] * GRID_W, GRID_W) for i in rng]
    qs = [_stack_heads(q_ref[0, i * GRID_W:(i + 1) * GRID_W, :] * (HEAD_DIM ** -0.5)) for i in rng]
    kb = [k_ref[0, pl.ds(start[i], band), :] for i in rng]
    vb = [v_ref[0, pl.ds(start[i], band), :] for i in rng]
    s_b = [_mm_nt(qs[i], kb[i]) + bias_ref[0, rs[i] - r[i] + NA_ROWS - 1] for i in rng]
    s_c = [_mm_nt(qs[i], kc) for i in rng]
    m = [jnp.maximum(jnp.max(s_b[i], axis=1, keepdims=True), jnp.max(s_c[i], axis=1, keepdims=True)) for i in rng]
    e_b = [jnp.exp(s_b[i] - m[i]) for i in rng]
    e_c = [jnp.exp(s_c[i] - m[i]) for i in rng]
    den = [jnp.sum(e_b[i], axis=1, keepdims=True) + jnp.sum(e_c[i], axis=1, keepdims=True) for i in rng]
    pv = [(_mm(e_b[i], vb[i]) + _mm(e_c[i], vc)) / den[i] for i in rng]
    for i in rng:
        rows_i = slice(i * GRID_W, (i + 1) * GRID_W)
        gate = _silu(g_ref[0, rows_i, :].astype(F32))
        o_ref[0, rows_i, :] = (_unstack_heads(pv[i], GRID_W) * gate).astype(o_ref.dtype)


def _natten_bias(rpb):
    depth = rpb.shape[0]
    c = np.arange(GRID_W)
    c_start = np.clip(c - NA_COLS // 2, 0, GRID_W - NA_COLS)
    col_ok = (c[None, :] >= c_start[:, None]) & (c[None, :] < c_start[:, None] + NA_COLS)
    dc = np.clip(c[None, :] - c[:, None], -(NA_COLS - 1), NA_COLS - 1) + NA_COLS - 1
    select = (np.arange(2 * NA_COLS - 1)[:, None, None] == dc[None]).astype(np.float32)
    rows = jnp.stack([rpb[:, :, o:o + NA_ROWS] for o in range(NA_ROWS)], axis=1)
    tab = jnp.einsum('lohic,cqk->lohqik', rows, jnp.asarray(select), precision=lax.Precision.HIGHEST)
    tab = jnp.where(jnp.asarray(col_ok)[None, None, None, :, None, :], tab, NEG)
    return tab.reshape(depth, NA_ROWS, N_HEADS * GRID_W, NA_ROWS * GRID_W)


def _natten(na, na_ctx, layer, bias):
    b, t, _ = na.shape
    lc = na_ctx.shape[1]
    rows = t // GRID_W
    band = NA_ROWS * GRID_W
    tq = NA_RT * GRID_W
    qrow = lambda c: (lambda i, r: (i, r, c))
    full = lambda c: (lambda i, r: (i, 0, c))
    return pl.pallas_call(
        functools.partial(_natten_kernel, rows=rows),
        grid=(b, rows // NA_RT),
        in_specs=[pl.BlockSpec((1, tq, BR_W), qrow(0)),
                  pl.BlockSpec((1, t, BR_W), full(1)),
                  pl.BlockSpec((1, t, BR_W), full(2)),
                  pl.BlockSpec((1, tq, BR_W), qrow(3)),
                  pl.BlockSpec((1, lc, BR_W), full(1)),
                  pl.BlockSpec((1, lc, BR_W), full(2)),
                  _layer_spec((NA_ROWS, N_HEADS * GRID_W, band), layer)],
        out_specs=pl.BlockSpec((1, tq, BR_W), qrow(0)),
        out_shape=jax.ShapeDtypeStruct((b, t, BR_W), BF16),
        compiler_params=_params("parallel", "arbitrary"),
        name="natten_mixer",
    )(na, na, na, na, na_ctx, na_ctx, bias)


def _ctx_attn_kernel(q_ref, k_ref, v_ref, g_ref, o_ref):
    n = q_ref.shape[1]
    qs = _stack_heads(q_ref[0] * (HEAD_DIM ** -0.5))
    s = _mm_nt(qs, k_ref[0])
    e = jnp.exp(s - jnp.max(s, axis=1, keepdims=True))
    pv = _mm(e, v_ref[0]) / jnp.sum(e, axis=1, keepdims=True)
    o_ref[0] = (_unstack_heads(pv, n) * _silu(g_ref[0].astype(F32))).astype(o_ref.dtype)


def _ctx_attention(na_ctx):
    b, lc, _ = na_ctx.shape
    tq = 64
    qrow = lambda c: (lambda i, j: (i, j, c))
    full = lambda c: (lambda i, j: (i, 0, c))
    return pl.pallas_call(
        _ctx_attn_kernel,
        grid=(b, lc // tq),
        in_specs=[pl.BlockSpec((1, tq, BR_W), qrow(0)),
                  pl.BlockSpec((1, lc, BR_W), full(1)),
                  pl.BlockSpec((1, lc, BR_W), full(2)),
                  pl.BlockSpec((1, tq, BR_W), qrow(3))],
        out_specs=pl.BlockSpec((1, tq, BR_W), qrow(0)),
        out_shape=jax.ShapeDtypeStruct((b, lc, BR_W), BF16),
        compiler_params=_params("parallel", "arbitrary"),
        name="ctx_attention",
    )(na_ctx, na_ctx, na_ctx, na_ctx)


def _stack_pair(x):
    head = _head_of_lane(x.shape)
    return jnp.concatenate([jnp.where(head == 0, x, 0.0), jnp.where(head == 1, x, 0.0)], axis=0)


def _scan_chunks(chains):
    n = chains[0][0].shape[0]
    m = 2 * n
    nc = len(chains)
    rng = range(nc)
    rev = [c[7] for c in chains]
    row_n = lax.broadcasted_iota(jnp.int32, (n, LANES), 0)
    i2 = lax.broadcasted_iota(jnp.int32, (m, 2 * m), 0)
    j2 = lax.broadcasted_iota(jnp.int32, (m, 2 * m), 1) % n
    t2 = i2 % n
    same = jnp.where(i2 < n, 0, 1)
    keep = {False: j2 < t2 + same, True: j2 > t2 - same}
    ri = lax.broadcasted_iota(jnp.int32, (LANES, LANES), 0)
    ci = lax.broadcasted_iota(jnp.int32, (LANES, LANES), 1)
    eye = ri == ci
    same_head = (ri // HEAD_DIM) == (ci // HEAD_DIM)
    d = functools.partial(jnp.dot, preferred_element_type=F32)

    def running_sum(x, reverse):
        s = 1
        while s < n:
            if reverse:
                x = x + jnp.where(row_n < n - s, pltpu.roll(x, n - s, 0), 0.0)
            else:
                x = x + jnp.where(row_n >= s, pltpu.roll(x, s, 0), 0.0)
            s *= 2
        return x

    cum = [running_sum(chains[c][3], rev[c]) for c in rng]
    tot = [jnp.sum(chains[c][3], axis=0, keepdims=True) for c in rng]
    e_neg = [jnp.exp(-cum[c]) for c in rng]
    ar = [jnp.concatenate([chains[c][2] * jnp.exp(cum[c] - chains[c][3]),
                           chains[c][0] * jnp.exp(cum[c])], axis=0).astype(BF16) for c in rng]
    bk_st = [jnp.concatenate([_stack_pair((chains[c][5] * e_neg[c]).astype(BF16)),
                              _stack_pair((chains[c][4] * e_neg[c]).astype(BF16))], axis=0) for c in rng]
    v_bf = [chains[c][1].astype(BF16) for c in rng]

    g = [jnp.where(keep[rev[c]], _mm_nt(ar[c], bk_st[c]), 0.0) for c in rng]
    lhs = [jnp.concatenate([ar[c], g[c][:, m:].astype(BF16)], axis=1) for c in rng]
    rhs = [jnp.concatenate([chains[c][6].astype(BF16), _stack_pair(v_bf[c])], axis=0) for c in rng]
    xy = [d(lhs[c], rhs[c]) for c in rng]

    nblk = n // SCAN_BLK
    row_t = lax.broadcasted_iota(jnp.int32, (n, LANES), 0)
    lane_s = lax.broadcasted_iota(jnp.int32, (n, LANES), 1) % n
    same_blk = row_t // SCAN_BLK == lane_s // SCAN_BLK
    p = [jnp.where(same_blk, g[c][:n, :m], 0.0) for c in rng]
    off = [jnp.where(same_blk, 0.0, g[c][:n, :m]).astype(BF16) for c in rng]
    y0 = [xy[c][:n] for c in rng]
    t_d = [jnp.where(row_t == lane_s, 1.0, 0.0) for c in rng]
    step = 1
    while step < SCAN_BLK:
        step *= 2
        last = step >= SCAN_BLK
        for c in rng:
            p_bf = p[c].astype(BF16)
            y_hi, y_lo = _split(y0[c])
            parts = [_stack_pair(y_hi), _stack_pair(y_lo), _stack_pair(t_d[c].astype(BF16))]
            if not last:
                parts.append(_stack_pair(p_bf))
            acc = d(p_bf, jnp.concatenate(parts, axis=1))
            y0[c] = y0[c] + (acc[:, :LANES] + acc[:, LANES:2 * LANES])
            t_d[c] = t_d[c] + acc[:, 2 * LANES:3 * LANES]
            p[c] = None if last else acc[:, 3 * LANES:]

    u = list(y0)
    for k in range(1, nblk):
        for c in rng:
            blk = nblk - 1 - k if rev[c] else k
            rows = slice(blk * SCAN_BLK, (blk + 1) * SCAN_BLK)
            u_hi, u_lo = _split(u[c])
            w = d(off[c][rows], jnp.concatenate([_stack_pair(u_hi), _stack_pair(u_lo)], axis=1))
            w = w[:, :LANES] + w[:, LANES:]
            w_hi, w_lo = _split(jnp.concatenate([w] * nblk, axis=0))
            v = d(t_d[c][rows].astype(BF16), jnp.concatenate([_stack_pair(w_hi), _stack_pair(w_lo)], axis=1))
            pieces = [u[c][:blk * SCAN_BLK]] if blk > 0 else []
            pieces.append(y0[c][rows] + (v[:, :LANES] + v[:, LANES:]))
            if blk < nblk - 1:
                pieces.append(u[c][(blk + 1) * SCAN_BLK:])
            u[c] = jnp.concatenate(pieces, axis=0)

    outs = []
    for c in rng:
        u_bf = u[c].astype(BF16)
        y = xy[c][n:] + d(g[c][n:, :m].astype(BF16), _stack_pair(u_bf))
        e_rest = jnp.exp(tot[c] - cum[c])
        bk_rest = jnp.concatenate([chains[c][5] * e_rest, chains[c][4] * e_rest], axis=0)
        decay = jnp.sum(jnp.where(eye, jnp.exp(tot[c]), 0.0), axis=1, keepdims=True)
        upd = _mm_tn(bk_rest, jnp.concatenate([u_bf, v_bf[c]], axis=0))
        outs.append((y, chains[c][6] * decay + jnp.where(same_head, upd, 0.0)))
    return outs


def _scan_kernel(sf_ref, df_ref, sb_ref, db_ref, s0_ref, yf_ref, yb_ref, sfin_ref, state_ref):
    j = pl.program_id(1)

    @pl.when(j == 0)
    def _():
        state_ref[...] = s0_ref[...]

    nb = sf_ref.shape[0]
    npair = BR_W // LANES
    chains = []
    for i in range(nb):
        for dr, (s_ref, d_ref) in enumerate(((sf_ref, df_ref), (sb_ref, db_ref))):
            for p in range(npair):
                col = lambda blk: slice(blk * BR_W + p * LANES, blk * BR_W + (p + 1) * LANES)
                chains.append((s_ref[i, :, col(PREP_R)], s_ref[i, :, col(PREP_V)], s_ref[i, :, col(PREP_A)],
                               d_ref[i, :, col(0)], d_ref[i, :, col(1)], d_ref[i, :, col(2)],
                               state_ref[i, dr, p], dr == 1))
    outs = iter(_scan_chunks(chains))
    for i in range(nb):
        for dr, y_ref in enumerate((yf_ref, yb_ref)):
            for p in range(npair):
                y, new_state = next(outs)
                y_ref[i, :, p * LANES:(p + 1) * LANES] = y
                state_ref[i, dr, p] = new_state

    @pl.when(j == pl.num_programs(1) - 1)
    def _():
        sfin_ref[...] = state_ref[...]


def _rwkv_scan(prep, s0):
    b, t, _ = prep.shape
    n = t // SCAN_L
    npair = BR_W // LANES
    blk3 = 3 * BR_W
    fwd = lambda c: (lambda i, j: (i, j, c))
    bwd = lambda c: (lambda i, j: (i, n - 1 - j, c))
    st = lambda i, j: (i, 0, 0, 0, 0)
    nb = SCAN_NB if b % SCAN_NB == 0 else 1
    return pl.pallas_call(
        _scan_kernel,
        grid=(b // nb, n),
        in_specs=[pl.BlockSpec((nb, SCAN_L, blk3), fwd(0)),
                  pl.BlockSpec((nb, SCAN_L, blk3), fwd(1)),
                  pl.BlockSpec((nb, SCAN_L, blk3), bwd(0)),
                  pl.BlockSpec((nb, SCAN_L, blk3), bwd(2)),
                  pl.BlockSpec((nb, 2, npair, LANES, LANES), st)],
        out_specs=[pl.BlockSpec((nb, SCAN_L, BR_W), fwd(0)),
                   pl.BlockSpec((nb, SCAN_L, BR_W), bwd(0)),
                   pl.BlockSpec((nb, 2, npair, LANES, LANES), st)],
        out_shape=[jax.ShapeDtypeStruct((b, t, BR_W), F32),
                   jax.ShapeDtypeStruct((b, t, BR_W), F32),
                   jax.ShapeDtypeStruct((b, 2, npair, LANES, LANES), F32)],
        scratch_shapes=[pltpu.VMEM((nb, 2, npair, LANES, LANES), F32)],
        compiler_params=_params("parallel", "arbitrary"),
        name="rwkv_scan",
    )(prep, prep, prep, prep, s0)


def _rwkv_finish(y_sum, bonus, gate, gn_g, gn_b):
    avg = _group_matrix(1.0 / HEAD_DIM)
    mu = _mm_exact_rhs(y_sum, avg, precise=True)
    oc = y_sum - mu
    var = _mm_exact_rhs(oc * oc, avg)
    o = oc * lax.rsqrt(var + GN_EPS) * gn_g + gn_b + bonus
    return o * _silu(gate)


def _merge_kernel(x_ref, g_ref, sh_ref, sc_ref, gt_ref, gm_ref, na_ref, yf_ref, yb_ref, bonus_ref, rwg_ref,
                  gng_ref, gnb_ref, pl_ref, wm_ref, wbr_ref, wout_ref, fin_ref, o_ref, *, final):
    x = x_ref[0]
    d = x.shape[-1]
    h = _norm_mod(x, g_ref[0], sh_ref[0, 0], sc_ref[0, 0]).astype(BF16)
    rw = _rwkv_finish(yf_ref[0] + yb_ref[0], bonus_ref[0], rwg_ref[0].astype(F32), gng_ref[0], gnb_ref[0])
    y = None
    for i, br in enumerate((gm_ref[0], na_ref[0], rw.astype(BF16), pl_ref[0])):
        logits = jnp.dot(h, wm_ref[0, :, i * d:(i + 1) * d], preferred_element_type=F32)
        term = _sigmoid_t(logits) * jnp.dot(br, wbr_ref[0, i], preferred_element_type=F32)
        y = term if y is None else y + term
    out = x + gt_ref[0, 0] * jnp.dot(y.astype(BF16), wout_ref[0], preferred_element_type=F32)
    if final:
        ms = jnp.mean(out * out, axis=-1, keepdims=True)
        out = out * lax.rsqrt(ms + EPS) * fin_ref[...]
    o_ref[0] = out


def _merge(x, layer, mod, mod_row, gm, na, yf, yb, prep, rwg, plb, p, final):
    b, t, d = x.shape
    tm = min(t, 512)
    row = lambda i, j: (i, j, 0)
    modspec = lambda c: pl.BlockSpec((1, 1, 1, d), lambda i, j: (layer, mod_row(i), 0, c))
    br = pl.BlockSpec((1, tm, BR_W), row)
    return pl.pallas_call(
        functools.partial(_merge_kernel, final=final),
        grid=(b, t // tm),
        in_specs=[pl.BlockSpec((1, tm, d), row), _layer_spec((1, d), layer), modspec(0), modspec(1), modspec(2),
                  br, br, br, br, pl.BlockSpec((1, tm, BR_W), lambda i, j: (i, j, PREP_BONUS)), br,
                  _layer_spec((1, BR_W), layer), _layer_spec((1, BR_W), layer), br,
                  _layer_spec((d, N_BRANCH * d), layer), _layer_spec((N_BRANCH, BR_W, d), layer),
                  _layer_spec((d, d), layer), pl.BlockSpec((1, d), lambda i, j: (0, 0))],
        out_specs=pl.BlockSpec((1, tm, d), row),
        out_shape=jax.ShapeDtypeStruct((b, t, d), F32),
        compiler_params=_params("parallel", "parallel"),
        name="merge_out_proj",
    )(x, p['norm_g'], mod, mod, mod, gm, na, yf, yb, prep, rwg, p['rw_gn_g'], p['rw_gn_b'], plb,
      p['w_merge'], p['w_br'], p['w_out'], p['final_g'])


def _layer(x, ctx, layer, mod, p, last):
    b = x.shape[0]
    latent_row = lambda i: i
    ctx_row = lambda i: b

    gm_c, na_c, prep_c, rwg_c, pl_c = _project_mix(ctx, layer, mod, ctx_row, p)
    gm_x, na_x, prep_x, rwg_x, pl_x = _project_mix(x, layer, mod, latent_row, p)

    zero_state = jnp.zeros((b, 2, BR_W // LANES, LANES, LANES), F32)
    yf_c, yb_c, s_ctx = _rwkv_scan(prep_c, zero_state)
    yf_x, yb_x, _ = _rwkv_scan(prep_x, s_ctx)

    x_new = _merge(x, layer, mod, latent_row, gm_x, _natten(na_x, na_c, layer, p['na_bias']),
                   yf_x, yb_x, prep_x, rwg_x, _pool(pl_x, layer, p), p, final=last)
    if last:
        return x_new, ctx
    ctx_new = _merge(ctx, layer, mod, ctx_row, gm_c, _ctx_attention(na_c),
                     yf_c, yb_c, prep_c, rwg_c, _pool(pl_c, layer, p), p, final=False)
    return x_new, ctx_new


def kernel(x, c, ctx, c_ctx, ada_w, ada_b, norm_g, w_in, gm_ln_g, gm_ln_b, gm_ws, gm_bs, na_rpb, rw_mu, rw_w0, rw_w2, rw_a0, rw_a2, rw_kk, rw_ka, rw_rk, rw_gn_g, rw_gn_b, pl_w, pl_scale, w_br, w_out, final_g):
    b, _, d = x.shape
    depth = ada_w.shape[0]
    mod_rows = 16
    cs = jnp.zeros((mod_rows, d), F32).at[:b].set(c).at[b].set(c_ctx)
    mod = _modulation(cs, ada_w, ada_b).reshape(depth, mod_rows, 1, 3 * d)

    vec = lambda a: a.reshape(depth, 1, -1)
    lora_zeros = jnp.zeros((depth, LORA, 2 * BR_W), F32)
    side = lambda w: jnp.concatenate([w[:, 0], w[:, 1]], axis=2)
    lora = jnp.concatenate([jnp.concatenate([side(rw_w2), lora_zeros], axis=2),
                            jnp.concatenate([lora_zeros, side(rw_a2)], axis=2)], axis=1).astype(BF16)
    p = {
        'norm_g': vec(norm_g), 'final_g': final_g.reshape(1, d),
        'w_mix': w_in[:, :, :MIX_COLS].astype(BF16), 'w_merge': w_in[:, :, MIX_COLS:].astype(BF16),
        'w_br': w_br.astype(BF16), 'w_out': w_out.astype(BF16),
        'gm_ln_g': vec(gm_ln_g), 'gm_ln_b': vec(gm_ln_b), 'gm_ws': gm_ws.transpose(0, 2, 1, 3).reshape(depth, CHUNK, N_HEADS * CHUNK).astype(BF16),
        'gm_bs': jnp.repeat(gm_bs.transpose(0, 2, 1), HEAD_DIM, axis=2),
        'na_bias': _natten_bias(na_rpb),
        'rw_mu': vec(rw_mu), 'rw_kk': vec(rw_kk), 'rw_ka': vec(rw_ka), 'rw_rk': vec(rw_rk),
        'rw_w0': rw_w0, 'rw_a0': rw_a0,
        'rw_lora': jnp.concatenate([lora, lora], axis=1),
        'rw_gn_g': vec(rw_gn_g), 'rw_gn_b': vec(rw_gn_b),
        'pl_w': jnp.einsum('lgcd,gh->lgchd', pl_w, jnp.eye(N_HEADS, dtype=F32)).reshape(depth, BR_W, BR_W).astype(BF16),
        'pl_scale': vec(pl_scale),
    }
    for layer in range(depth):
        x, ctx = _layer(x, ctx, layer, mod, p, last=(layer == depth - 1))
    return x
```
